```python
import math
import jax, jax.numpy as jnp
from jax import lax
import numpy as np

D_MODEL = 1024
BATCH = 16
SEQ = 2048
DEPTH = 1

CHUNK = 64
MIX_WIDTH = D_MODEL
SB_WIDTH = MIX_WIDTH // 2
SB_HEAD_DIM = 64
SB_HEADS = SB_WIDTH // SB_HEAD_DIM
SSM_WIDTH = MIX_WIDTH - SB_WIDTH
SSM_GROUP = 16
SSM_GROUPS = SSM_WIDTH // SSM_GROUP
SSM_STATE = 64
D_FF = 4 * D_MODEL
QBLOCK = 128
EPS = 1e-6
DT_MIN = 1e-3
DT_MAX = 1e-1

kernel_name = "hybrid_stickbreaking_s5_block"


def rmsnorm(x, g):
    xf = x.astype(jnp.float32)
    y = xf * lax.rsqrt(jnp.mean(xf * xf, axis=-1, keepdims=True) + EPS)
    return y * g.astype(jnp.float32)


def stick_breaking_attention(q, k, v):
    L = q.shape[2]
    scale = 1.0 / math.sqrt(q.shape[-1])
    outs = []
    for i in range(L // QBLOCK):
        q0 = i * QBLOCK
        kend = q0 + QBLOCK
        qb = q[:, :, q0:kend]
        kb = k[:, :, :kend]
        vb = v[:, :, :kend]
        z = jnp.einsum('bhqd,bhkd->bhqk', qb, kb) * scale
        t_idx = q0 + jnp.arange(QBLOCK)[:, None]
        s_idx = jnp.arange(kend)[None, :]
        mask = s_idx < t_idx
        log_one_minus = jnp.where(mask, -jax.nn.softplus(z), 0.0)
        tail = lax.cumsum(log_one_minus, axis=3, reverse=True) - log_one_minus
        log_a = jax.nn.log_sigmoid(z) + tail
        a = jnp.where(mask, jnp.exp(log_a), 0.0)
        outs.append(jnp.einsum('bhqk,bhkd->bhqd', a, vb))
    return jnp.concatenate(outs, axis=2)


def s5_glu(u, lam_re, lam_im, log_dt, b_re, b_im, c_re, c_im, d_skip, w_glu, b_glu):
    Bsz, L, _ = u.shape
    ug = u.reshape(Bsz, L, SSM_GROUPS, SSM_GROUP)
    lam = lax.complex(lam_re.astype(jnp.float32), lam_im.astype(jnp.float32))
    dt = jnp.exp(log_dt.astype(jnp.float32))[:, None]
    lam_bar = jnp.exp(lam * dt)
    b_mat = lax.complex(b_re.astype(jnp.float32), b_im.astype(jnp.float32))
    c_mat = lax.complex(c_re.astype(jnp.float32), c_im.astype(jnp.float32))
    b_bar = ((lam_bar - 1.0) / lam)[:, :, None] * b_mat
    bu = jnp.einsum('blgh,gph->blgp', ug.astype(jnp.complex64), b_bar)
    lam_seq = jnp.broadcast_to(lam_bar[None, None], (1, L, SSM_GROUPS, SSM_STATE))

    def combine(e_i, e_j):
        a_i, s_i = e_i
        a_j, s_j = e_j
        return a_j * a_i, a_j * s_i + s_j

    _, states = lax.associative_scan(combine, (lam_seq, bu), axis=1)
    y = jnp.einsum('blgp,ghp->blgh', states, c_mat).real + d_skip.astype(jnp.float32)[None, None] * ug
    y = jax.nn.gelu(y.reshape(Bsz, L, SSM_WIDTH))
    gate = jax.nn.sigmoid(y @ w_glu.astype(jnp.float32) + b_glu.astype(jnp.float32))
    return y * gate


def setup_inputs(seed: int = 0) -> dict:
    key = jax.random.key(seed)
    ks = jax.random.split(key, 24)
    f32 = jnp.float32
    G, P, H = SSM_GROUPS, SSM_STATE, SSM_GROUP
    x = jax.random.normal(ks[0], (BATCH, SEQ, D_MODEL), f32)
    norm1_g = 1.0 + 0.02 * jax.random.normal(ks[1], (D_MODEL,), f32)
    w_in = jax.random.normal(ks[2], (D_MODEL, 3 * SB_WIDTH + SSM_WIDTH), f32) * D_MODEL ** -0.5
    q_norm_g = 1.0 + 0.02 * jax.random.normal(ks[3], (SB_HEAD_DIM,), f32)
    k_norm_g = 1.0 + 0.02 * jax.random.normal(ks[4], (SB_HEAD_DIM,), f32)
    ssm_lambda_re = -0.5 + 0.01 * jax.random.normal(ks[5], (G, P), f32)
    ssm_lambda_im = math.pi * jnp.broadcast_to(jnp.arange(P, dtype=f32)[None], (G, P)) \
        + 0.01 * jax.random.normal(ks[6], (G, P), f32)
    ssm_log_dt = jax.random.uniform(ks[7], (G,), f32, math.log(DT_MIN), math.log(DT_MAX))
    ssm_b_re = jax.random.normal(ks[8], (G, P, H), f32) * (2.0 * H) ** -0.5
    ssm_b_im = jax.random.normal(ks[9], (G, P, H), f32) * (2.0 * H) ** -0.5
    ssm_c_re = jax.random.normal(ks[10], (G, H, P), f32) * (2.0 * P) ** -0.5
    ssm_c_im = jax.random.normal(ks[11], (G, H, P), f32) * (2.0 * P) ** -0.5
    ssm_d = jax.random.normal(ks[12], (G, H), f32)
    w_glu = jax.random.normal(ks[13], (SSM_WIDTH, SSM_WIDTH), f32) * SSM_WIDTH ** -0.5
    b_glu = 0.01 * jax.random.normal(ks[14], (SSM_WIDTH,), f32)
    attn_out_g = 1.0 + 0.02 * jax.random.normal(ks[15], (SB_WIDTH,), f32)
    ssm_out_g = 1.0 + 0.02 * jax.random.normal(ks[16], (SSM_WIDTH,), f32)
    w_out = jax.random.normal(ks[17], (MIX_WIDTH, D_MODEL), f32) * MIX_WIDTH ** -0.5
    norm2_g = 1.0 + 0.02 * jax.random.normal(ks[18], (D_MODEL,), f32)
    w_mlp_in = jax.random.normal(ks[19], (D_MODEL, D_FF), f32) * D_MODEL ** -0.5
    w_mlp_out = jax.random.normal(ks[20], (D_FF, D_MODEL), f32) * D_FF ** -0.5
    return {"x": x, "norm1_g": norm1_g, "w_in": w_in, "q_norm_g": q_norm_g, "k_norm_g": k_norm_g,
            "ssm_lambda_re": ssm_lambda_re, "ssm_lambda_im": ssm_lambda_im, "ssm_log_dt": ssm_log_dt,
            "ssm_b_re": ssm_b_re, "ssm_b_im": ssm_b_im, "ssm_c_re": ssm_c_re, "ssm_c_im": ssm_c_im,
            "ssm_d": ssm_d, "w_glu": w_glu, "b_glu": b_glu, "attn_out_g": attn_out_g,
            "ssm_out_g": ssm_out_g, "w_out": w_out, "norm2_g": norm2_g,
            "w_mlp_in": w_mlp_in, "w_mlp_out": w_mlp_out}


def reference(x, norm1_g, w_in, q_norm_g, k_norm_g, ssm_lambda_re, ssm_lambda_im, ssm_log_dt,
              ssm_b_re, ssm_b_im, ssm_c_re, ssm_c_im, ssm_d, w_glu, b_glu, attn_out_g,
              ssm_out_g, w_out, norm2_g, w_mlp_in, w_mlp_out):
    Bsz, L, _ = x.shape
    h = x.astype(jnp.float32)
    for _layer in range(DEPTH):
        xn = rmsnorm(h, norm1_g)
        proj = xn @ w_in.astype(jnp.float32)
        q, k, v, u = jnp.split(proj, [SB_WIDTH, 2 * SB_WIDTH, 3 * SB_WIDTH], axis=-1)

        def heads(t):
            return t.reshape(Bsz, L, SB_HEADS, SB_HEAD_DIM).transpose(0, 2, 1, 3)

        qh = rmsnorm(heads(q), q_norm_g)
        kh = rmsnorm(heads(k), k_norm_g)
        vh = heads(v)
        sb = stick_breaking_attention(qh, kh, vh)
        sb = sb.transpose(0, 2, 1, 3).reshape(Bsz, L, SB_WIDTH)

        ssm = s5_glu(u, ssm_lambda_re, ssm_lambda_im, ssm_log_dt, ssm_b_re, ssm_b_im,
                     ssm_c_re, ssm_c_im, ssm_d, w_glu, b_glu)

        mixed = jnp.concatenate([rmsnorm(sb, attn_out_g), rmsnorm(ssm, ssm_out_g)], axis=-1)
        h = h + mixed @ w_out.astype(jnp.float32)

        hn = rmsnorm(h, norm2_g)
        a = jnp.square(jax.nn.relu(hn @ w_mlp_in.astype(jnp.float32)))
        h = h + a @ w_mlp_out.astype(jnp.float32)
    return h.astype(x.dtype)
```

```python
import functools
import math

import jax
import jax.numpy as jnp
from jax import lax
from jax.experimental import pallas as pl
from jax.experimental.pallas import tpu as pltpu

F32 = jnp.float32
BF16 = jnp.bfloat16

D_MODEL = 1024
SB_WIDTH = 512
HEAD_DIM = 64
HEAD_PAIRS = SB_WIDTH // (2 * HEAD_DIM)
SSM_WIDTH = 512
SSM_GROUP = 16
SSM_GROUPS = 32
SSM_STATE = 64
SSM_REAL = SSM_GROUPS * SSM_STATE
D_FF = 4 * D_MODEL
EPS = 1e-6

LANES = 128
MXU_DIM = 256
VMEM_LIMIT_BYTES = 56 * 1024 * 1024

TM_PROJ = 512
TQ = 256
TK = 128
TT = 32
SCAN_W = 512
TM_MLP = 512
FF_CHUNK = 1024

SKIP_THRESHOLD = 104.0


def _rms(x):
    return x * lax.rsqrt(jnp.mean(x * x, axis=-1, keepdims=True) + EPS)


def _inproj_kernel(x_ref, g1_ref, win_ref, hsel_ref, gq_ref, gk_ref,
                   qt_ref, k_ref, vt_ref, u_ref):
    x = x_ref[0]
    xn = (_rms(x) * g1_ref[...]).astype(BF16)
    proj = jnp.dot(xn, win_ref[...], preferred_element_type=F32)
    q = proj[:, 0 * SB_WIDTH:1 * SB_WIDTH]
    k = proj[:, 1 * SB_WIDTH:2 * SB_WIDTH]
    v = proj[:, 2 * SB_WIDTH:3 * SB_WIDTH]
    u = proj[:, 3 * SB_WIDTH:]
    hsel = hsel_ref[...]
    msq = jnp.dot((q * q).astype(BF16), hsel, preferred_element_type=F32)
    msk = jnp.dot((k * k).astype(BF16), hsel, preferred_element_type=F32)
    qn = q * lax.rsqrt(msq + EPS) * gq_ref[...]
    kn = k * lax.rsqrt(msk + EPS) * gk_ref[...]
    k_ref[0] = kn.astype(BF16)
    u_ref[...] = u.astype(BF16)
    for j in range(TM_PROJ // TQ):
        qt = qn[j * TQ:(j + 1) * TQ, :].T
        qt_ref[0, :, j] = qt.astype(BF16).reshape(HEAD_PAIRS, LANES, TQ)
    for j in range(TM_PROJ // TK):
        vt = v[j * TK:(j + 1) * TK, :].T
        vt_ref[0, :, j] = vt.astype(BF16).reshape(HEAD_PAIRS, LANES, TK)


def _inproj(x, g1, win, hsel, gq, gk):
    B, L, D = x.shape
    nq, nk = L // TQ, L // TK
    const = lambda b, t: (0, 0)
    return pl.pallas_call(
        _inproj_kernel,
        grid=(B, L // TM_PROJ),
        in_specs=[
            pl.BlockSpec((1, TM_PROJ, D), lambda b, t: (b, t, 0)),
            pl.BlockSpec((1, D), const),
            pl.BlockSpec((D, 4 * SB_WIDTH), const),
            pl.BlockSpec((SB_WIDTH, SB_WIDTH), const),
            pl.BlockSpec((1, SB_WIDTH), const),
            pl.BlockSpec((1, SB_WIDTH), const),
        ],
        out_specs=[
            pl.BlockSpec((1, HEAD_PAIRS, TM_PROJ // TQ, LANES, TQ), lambda b, t: (b, 0, t, 0, 0)),
            pl.BlockSpec((1, TM_PROJ, SB_WIDTH), lambda b, t: (b, t, 0)),
            pl.BlockSpec((1, HEAD_PAIRS, TM_PROJ // TK, LANES, TK), lambda b, t: (b, 0, t, 0, 0)),
            pl.BlockSpec((TM_PROJ, SSM_WIDTH), lambda b, t: (t, b)),
        ],
        out_shape=[
            jax.ShapeDtypeStruct((B, HEAD_PAIRS, nq, LANES, TQ), BF16),
            jax.ShapeDtypeStruct((B, L, SB_WIDTH), BF16),
            jax.ShapeDtypeStruct((B, HEAD_PAIRS, nk, LANES, TK), BF16),
            jax.ShapeDtypeStruct((L, B * SSM_WIDTH), BF16),
        ],
        compiler_params=pltpu.CompilerParams(
            dimension_semantics=("parallel", "parallel"),
            vmem_limit_bytes=VMEM_LIMIT_BYTES),
        name="inproj",
    )(x, g1, win, hsel, gq, gk)


def _attn_kernel(qt_ref, k_ref, vt_ref, o_ref, acc_ref, r_ref):
    nq = qt_ref.shape[2]
    row = lax.broadcasted_iota(jnp.int32, (TK, TQ), 0)
    col = lax.broadcasted_iota(jnp.int32, (TK, TQ), 1)
    urow = lax.broadcasted_iota(jnp.int32, (TK, TK), 0)
    ucol = lax.broadcasted_iota(jnp.int32, (TK, TK), 1)
    tri = jnp.where(ucol >= urow, 1.0, 0.0).astype(BF16)
    drow = lax.broadcasted_iota(jnp.int32, (LANES, TQ), 0)

    def q_tile(qi, carry):
        qt = qt_ref[0, 0, qi]
        zero = jnp.zeros_like(qt)
        qh = (jnp.where(drow < HEAD_DIM, qt, zero), jnp.where(drow >= HEAD_DIM, qt, zero))
        acc_ref[...] = jnp.zeros_like(acc_ref)
        r_ref[...] = jnp.zeros_like(r_ref)

        def block(kb, mask):
            kblk = k_ref[0, pl.ds(pl.multiple_of(kb * TK, TK), TK), :]
            vblk = vt_ref[0, 0, kb]
            rmin = None
            for h in range(2):
                z = jnp.dot(kblk, qh[h], preferred_element_type=F32)
                sp = jnp.maximum(z, 0.0) + jnp.log(1.0 + jnp.exp(-jnp.abs(z)))
                if mask is not None:
                    sp = jnp.where(mask, sp, 0.0)
                c = jnp.dot(tri, sp.astype(BF16), preferred_element_type=F32)
                r_old = r_ref[h]
                a = jnp.exp(z - (c + r_old))
                if mask is not None:
                    a = jnp.where(mask, a, 0.0)
                acc_ref[h] += jnp.dot(vblk[h * HEAD_DIM:(h + 1) * HEAD_DIM, :], a.astype(BF16),
                                      preferred_element_type=F32)
                r_new = r_old + c[0:1, :]
                r_ref[h] = r_new
                m = jnp.min(r_new)
                rmin = m if rmin is None else jnp.minimum(rmin, m)
            return rmin

        block(2 * qi + 1, (row + TK) < col)
        rmin0 = block(2 * qi, row < col)

        def cond(c):
            kb, rmin = c
            return jnp.logical_and(kb >= 0, rmin < SKIP_THRESHOLD)

        def body(c):
            kb, _ = c
            return kb - 1, block(kb, None)

        lax.while_loop(cond, body, (2 * qi - 1, rmin0))
        acc = jnp.concatenate([acc_ref[0], acc_ref[1]], axis=0)
        o_ref[0, pl.ds(pl.multiple_of(qi * TQ, TQ), TQ), :] = acc.T
        return carry

    lax.fori_loop(0, nq, q_tile, 0)


def _attention(qt, k, vt):
    B, _, nq, _, _ = qt.shape
    L = k.shape[1]
    nk = vt.shape[2]
    return pl.pallas_call(
        _attn_kernel,
        grid=(B, HEAD_PAIRS),
        in_specs=[
            pl.BlockSpec((1, 1, nq, LANES, TQ), lambda b, p: (b, p, 0, 0, 0)),
            pl.BlockSpec((1, L, LANES), lambda b, p: (b, 0, p)),
            pl.BlockSpec((1, 1, nk, LANES, TK), lambda b, p: (b, p, 0, 0, 0)),
        ],
        out_specs=pl.BlockSpec((1, L, LANES), lambda b, p: (b, 0, p)),
        out_shape=jax.ShapeDtypeStruct((B, L, SB_WIDTH), F32),
        scratch_shapes=[
            pltpu.VMEM((2, HEAD_DIM, TQ), F32),
            pltpu.VMEM((2, 1, TQ), F32),
        ],
        compiler_params=pltpu.CompilerParams(
            dimension_semantics=("parallel", "parallel"),
            vmem_limit_bytes=VMEM_LIMIT_BYTES),
        name="sb_attention",
    )(qt, k, vt)


def _gelu_tanh(x):
    c = math.sqrt(2.0 / math.pi)
    return 0.5 * x * (1.0 + jnp.tanh(c * (x + 0.044715 * (x * x * x))))


def _s5_kernel(u_ref, bre_ref, bim_ref, are_ref, aim_ref, cre_ref, cim_ref, d_ref,
               wglu_ref, bglu_ref, gs_ref, o_ref, bu_ref, xs_ref, st_ref, *, batch):
    @pl.when(pl.program_id(0) == 0)
    def _():
        st_ref[...] = jnp.zeros_like(st_ref)

    u = u_ref[...]
    half = SSM_REAL // 2
    for kb in range(2):
        ub = u[:, kb * MXU_DIM:(kb + 1) * MXU_DIM]
        bu_ref[:, kb * half:(kb + 1) * half] = jnp.dot(
            ub, bre_ref[kb], preferred_element_type=F32)
        bu_ref[:, SSM_REAL + kb * half:SSM_REAL + (kb + 1) * half] = jnp.dot(
            ub, bim_ref[kb], preferred_element_type=F32)

    for sc in range(SSM_REAL // SCAN_W):
        re = slice(sc * SCAN_W, (sc + 1) * SCAN_W)
        im = slice(SSM_REAL + sc * SCAN_W, SSM_REAL + (sc + 1) * SCAN_W)
        ar = jnp.broadcast_to(are_ref[:, re], (batch, SCAN_W))
        ai = jnp.broadcast_to(aim_ref[:, re], (batch, SCAN_W))

        def step(t, carry, re=re, im=im, ar=ar, ai=ai):
            xr, xi = carry
            rows = pl.ds(pl.multiple_of(t * batch, batch), batch)
            nxr = ar * xr - ai * xi + bu_ref[rows, re]
            nxi = ar * xi + ai * xr + bu_ref[rows, im]
            xs_ref[rows, re] = nxr.astype(BF16)
            xs_ref[rows, im] = nxi.astype(BF16)
            return nxr, nxi

        xr, xi = lax.fori_loop(0, TT, step, (st_ref[:, re], st_ref[:, im]), unroll=4)
        st_ref[:, re] = xr
        st_ref[:, im] = xi

    ys = []
    for ob in range(2):
        y = jnp.dot(xs_ref[:, ob * half:(ob + 1) * half], cre_ref[ob],
                    preferred_element_type=F32)
        y = y + jnp.dot(xs_ref[:, SSM_REAL + ob * half:SSM_REAL + (ob + 1) * half], cim_ref[ob],
                        preferred_element_type=F32)
        ys.append(y)
    y = jnp.concatenate(ys, axis=1) + d_ref[...] * u.astype(F32)
    y = _gelu_tanh(y)
    gate = jax.nn.sigmoid(
        jnp.dot(y.astype(BF16), wglu_ref[...], preferred_element_type=F32) + bglu_ref[...])
    o_ref[...] = (_rms(y * gate) * gs_ref[...]).astype(o_ref.dtype)


def _s5(u_tb, batch, bre, bim, are, aim, cre, cim, d, wglu, bglu, gs):
    rows_total = u_tb.shape[0]
    rows = TT * batch
    half = SSM_REAL // 2
    c2 = lambda i: (0, 0)
    c3 = lambda i: (0, 0, 0)
    return pl.pallas_call(
        functools.partial(_s5_kernel, batch=batch),
        grid=(rows_total // rows,),
        in_specs=[
            pl.BlockSpec((rows, SSM_WIDTH), lambda i: (i, 0)),
            pl.BlockSpec((2, MXU_DIM, half), c3),
            pl.BlockSpec((2, MXU_DIM, half), c3),
            pl.BlockSpec((1, SSM_REAL), c2),
            pl.BlockSpec((1, SSM_REAL), c2),
            pl.BlockSpec((2, half, MXU_DIM), c3),
            pl.BlockSpec((2, half, MXU_DIM), c3),
            pl.BlockSpec((1, SSM_WIDTH), c2),
            pl.BlockSpec((SSM_WIDTH, SSM_WIDTH), c2),
            pl.BlockSpec((1, SSM_WIDTH), c2),
            pl.BlockSpec((1, SSM_WIDTH), c2),
        ],
        out_specs=pl.BlockSpec((rows, SSM_WIDTH), lambda i: (i, 0)),
        out_shape=jax.ShapeDtypeStruct((rows_total, SSM_WIDTH), BF16),
        scratch_shapes=[
            pltpu.VMEM((rows, 2 * SSM_REAL), F32),
            pltpu.VMEM((rows, 2 * SSM_REAL), BF16),
            pltpu.VMEM((batch, 2 * SSM_REAL), F32),
        ],
        compiler_params=pltpu.CompilerParams(
            dimension_semantics=("arbitrary",),
            vmem_limit_bytes=VMEM_LIMIT_BYTES),
        name="s5_glu",
    )(u_tb, bre, bim, are, aim, cre, cim, d, wglu, bglu, gs)


def _out_mlp_kernel(x_ref, sb_ref, ssm_ref, ga_ref, wout_ref, g2_ref, w1_ref, w2_ref, o_ref):
    x = x_ref[0]
    an = (_rms(sb_ref[0]) * ga_ref[...]).astype(BF16)
    h = x + jnp.dot(an, wout_ref[:SB_WIDTH, :], preferred_element_type=F32)
    h = h + jnp.dot(ssm_ref[...], wout_ref[SB_WIDTH:, :], preferred_element_type=F32)
    hn = (_rms(h) * g2_ref[...]).astype(BF16)
    o_ref[0] = h
    for c in range(D_FF // FF_CHUNK):
        a = jnp.dot(hn, w1_ref[:, c * FF_CHUNK:(c + 1) * FF_CHUNK], preferred_element_type=F32)
        a = jnp.square(jnp.maximum(a, 0.0)).astype(BF16)
        o_ref[0] += jnp.dot(a, w2_ref[c * FF_CHUNK:(c + 1) * FF_CHUNK, :],
                            preferred_element_type=F32)


def _out_mlp(x, sb, ssm_tb, ga, wout, g2, w1, w2):
    B, L, D = x.shape
    const = lambda b, t: (0, 0)
    resident = functools.partial(pl.BlockSpec, index_map=const)
    return pl.pallas_call(
        _out_mlp_kernel,
        grid=(B, L // TM_MLP),
        in_specs=[
            pl.BlockSpec((1, TM_MLP, D), lambda b, t: (b, t, 0)),
            pl.BlockSpec((1, TM_MLP, SB_WIDTH), lambda b, t: (b, t, 0)),
            pl.BlockSpec((TM_MLP, SSM_WIDTH), lambda b, t: (t, b)),
            pl.BlockSpec((1, SB_WIDTH), const),
            resident((D, D)),
            pl.BlockSpec((1, D), const),
            resident((D, D_FF)),
            resident((D_FF, D)),
        ],
        out_specs=pl.BlockSpec((1, TM_MLP, D), lambda b, t: (b, t, 0)),
        out_shape=jax.ShapeDtypeStruct((B, L, D), F32),
        compiler_params=pltpu.CompilerParams(
            dimension_semantics=("parallel", "parallel"),
            vmem_limit_bytes=VMEM_LIMIT_BYTES),
        name="out_mlp",
    )(x, sb, ssm_tb, ga, wout, g2, w1, w2)


def _s5_params(lam_re, lam_im, log_dt, b_re, b_im, c_re, c_im):
    G, P, H = SSM_GROUPS, SSM_STATE, SSM_GROUP
    lr, li = lam_re.astype(F32), lam_im.astype(F32)
    dt = jnp.exp(log_dt.astype(F32))[:, None]
    mag = jnp.exp(lr * dt)
    are, aim = mag * jnp.cos(li * dt), mag * jnp.sin(li * dt)
    den = lr * lr + li * li
    wr = ((are - 1.0) * lr + aim * li) / den
    wi = (aim * lr - (are - 1.0) * li) / den
    bbr = wr[:, :, None] * b_re.astype(F32) - wi[:, :, None] * b_im.astype(F32)
    bbi = wr[:, :, None] * b_im.astype(F32) + wi[:, :, None] * b_re.astype(F32)
    are, aim = are.reshape(1, G * P), aim.reshape(1, G * P)
    gpb = MXU_DIM // H
    eye = jnp.eye(gpb, dtype=F32)

    def b_layout(b):
        b = b.reshape(G // gpb, gpb, P, H)
        return jnp.einsum("kgph,gf->kghfp", b, eye).reshape(G // gpb, gpb * H, gpb * P).astype(BF16)

    def c_layout(c):
        c = c.reshape(G // gpb, gpb, H, P)
        return jnp.einsum("kghp,gf->kgpfh", c, eye).reshape(G // gpb, gpb * P, gpb * H).astype(BF16)

    return (b_layout(bbr), b_layout(bbi), are, aim,
            c_layout(c_re.astype(F32)), c_layout(-c_im.astype(F32)))


def kernel(x, norm1_g, w_in, q_norm_g, k_norm_g, ssm_lambda_re, ssm_lambda_im, ssm_log_dt,
           ssm_b_re, ssm_b_im, ssm_c_re, ssm_c_im, ssm_d, w_glu, b_glu, attn_out_g,
           ssm_out_g, w_out, norm2_g, w_mlp_in, w_mlp_out):
    B, L, D = x.shape
    assert (D, L % TM_PROJ, L % TM_MLP, L % TT) == (D_MODEL, 0, 0, 0)
    heads = SB_WIDTH // HEAD_DIM
    row = lambda g: g.astype(F32).reshape(1, -1)

    head_id = jnp.arange(SB_WIDTH) // HEAD_DIM
    hsel = (head_id[:, None] == head_id[None, :]).astype(BF16) * (1.0 / HEAD_DIM)
    gq = row(jnp.tile(q_norm_g.astype(F32), heads)) * (1.0 / math.sqrt(HEAD_DIM))
    gk = row(jnp.tile(k_norm_g.astype(F32), heads))

    qt, k, vt, u_tb = _inproj(x.astype(F32), row(norm1_g), w_in.astype(BF16), hsel, gq, gk)
    sb = _attention(qt, k, vt)

    bre, bim, are, aim, cre, cim = _s5_params(
        ssm_lambda_re, ssm_lambda_im, ssm_log_dt, ssm_b_re, ssm_b_im, ssm_c_re, ssm_c_im)
    ssm_tb = _s5(u_tb.reshape(L * B, SSM_WIDTH), B, bre, bim, are, aim, cre, cim,
                 row(ssm_d), w_glu.astype(BF16), row(b_glu), row(ssm_out_g))

    out = _out_mlp(x.astype(F32), sb, ssm_tb.reshape(L, B * SSM_WIDTH), row(attn_out_g),
                   w_out.astype(BF16), row(norm2_g), w_mlp_in.astype(BF16), w_mlp_out.astype(BF16))
    return out.astype(x.dtype)
```

```python
import functools
import math

import jax
import jax.numpy as jnp
from jax import lax
from jax.experimental import pallas as pl
from jax.experimental.pallas import tpu as pltpu

F32 = jnp.float32
BF16 = jnp.bfloat16

D_MODEL = 1024
SB_WIDTH = 512
HEAD_DIM = 64
HEAD_PAIRS = SB_WIDTH // (2 * HEAD_DIM)
SSM_WIDTH = 512
SSM_GROUP = 16
SSM_GROUPS = 32
SSM_STATE = 64
SSM_REAL = SSM_GROUPS * SSM_STATE
D_FF = 4 * D_MODEL
EPS = 1e-6

LANES = 128
MXU_DIM = 256
VMEM_LIMIT_BYTES = 56 * 1024 * 1024

TM_PROJ = 512
TQ = 256
TK = 128
TT = 32
SCAN_W = 512
TM_MLP = 512
FF_CHUNK = 1024

SKIP_THRESHOLD = 104.0


def _rms(x):
    return x * lax.rsqrt(jnp.mean(x * x, axis=-1, keepdims=True) + EPS)


def _inproj_kernel(x_ref, g1_ref, win_ref, hsel_ref, gq_ref, gk_ref,
                   qt_ref, k_ref, vt_ref, u_ref):
    x = x_ref[0]
    xn = (_rms(x) * g1_ref[...]).astype(BF16)
    proj = jnp.dot(xn, win_ref[...], preferred_element_type=F32)
    q = proj[:, 0 * SB_WIDTH:1 * SB_WIDTH]
    k = proj[:, 1 * SB_WIDTH:2 * SB_WIDTH]
    v = proj[:, 2 * SB_WIDTH:3 * SB_WIDTH]
    u = proj[:, 3 * SB_WIDTH:]
    hsel = hsel_ref[...]
    msq = jnp.dot((q * q).astype(BF16), hsel, preferred_element_type=F32)
    msk = jnp.dot((k * k).astype(BF16), hsel, preferred_element_type=F32)
    qn = q * lax.rsqrt(msq + EPS) * gq_ref[...]
    kn = k * lax.rsqrt(msk + EPS) * gk_ref[...]
    k_ref[0] = kn.astype(BF16)
    u_ref[0] = u.astype(BF16)
    for j in range(TM_PROJ // TQ):
        qt = qn[j * TQ:(j + 1) * TQ, :].T
        qt_ref[0, :, j] = qt.astype(BF16).reshape(HEAD_PAIRS, LANES, TQ)
    for j in range(TM_PROJ // TK):
        vt = v[j * TK:(j + 1) * TK, :].T
        vt_ref[0, :, j] = vt.astype(BF16).reshape(HEAD_PAIRS, LANES, TK)


def _inproj(x, g1, win, hsel, gq, gk):
    B, L, D = x.shape
    nq, nk = L // TQ, L // TK
    const = lambda b, t: (0, 0)
    return pl.pallas_call(
        _inproj_kernel,
        grid=(B, L // TM_PROJ),
        in_specs=[
            pl.BlockSpec((1, TM_PROJ, D), lambda b, t: (b, t, 0)),
            pl.BlockSpec((1, D), const),
            pl.BlockSpec((D, 4 * SB_WIDTH), const),
            pl.BlockSpec((SB_WIDTH, SB_WIDTH), const),
            pl.BlockSpec((1, SB_WIDTH), const),
            pl.BlockSpec((1, SB_WIDTH), const),
        ],
        out_specs=[
            pl.BlockSpec((1, HEAD_PAIRS, TM_PROJ // TQ, LANES, TQ), lambda b, t: (b, 0, t, 0, 0)),
            pl.BlockSpec((1, TM_PROJ, SB_WIDTH), lambda b, t: (b, t, 0)),
            pl.BlockSpec((1, HEAD_PAIRS, TM_PROJ // TK, LANES, TK), lambda b, t: (b, 0, t, 0, 0)),
            pl.BlockSpec((1, TM_PROJ, SSM_WIDTH), lambda b, t: (b, t, 0)),
        ],
        out_shape=[
            jax.ShapeDtypeStruct((B, HEAD_PAIRS, nq, LANES, TQ), BF16),
            jax.ShapeDtypeStruct((B, L, SB_WIDTH), BF16),
            jax.ShapeDtypeStruct((B, HEAD_PAIRS, nk, LANES, TK), BF16),
            jax.ShapeDtypeStruct((B, L, SSM_WIDTH), BF16),
        ],
        compiler_params=pltpu.CompilerParams(
            dimension_semantics=("parallel", "parallel"),
            vmem_limit_bytes=VMEM_LIMIT_BYTES),
        name="inproj",
    )(x, g1, win, hsel, gq, gk)


def _attn_kernel(qt_ref, k_ref, vt_ref, o_ref, acc_ref, r_ref):
    nq = qt_ref.shape[2]
    row = lax.broadcasted_iota(jnp.int32, (TK, TQ), 0)
    col = lax.broadcasted_iota(jnp.int32, (TK, TQ), 1)
    mask_lo = row < col
    srow = lax.broadcasted_iota(jnp.int32, (TK, TK), 0)
    scol = lax.broadcasted_iota(jnp.int32, (TK, TK), 1)
    mask_top = srow < scol
    tri = jnp.where(scol >= srow, 1.0, 0.0).astype(BF16)
    drow = lax.broadcasted_iota(jnp.int32, (LANES, TQ), 0)

    def k_block(kb):
        return k_ref[0, pl.ds(pl.multiple_of(kb * TK, TK), TK), :]

    def scores(kblk, q, mask):
        z = jnp.dot(kblk, q, preferred_element_type=F32)
        sp = jnp.maximum(z, 0.0) + jnp.log(1.0 + jnp.exp(-jnp.abs(z)))
        if mask is not None:
            sp = jnp.where(mask, sp, 0.0)
        return z, jnp.dot(tri, sp.astype(BF16), preferred_element_type=F32)

    def head_queries(qi):
        qt = qt_ref[0, 0, qi]
        zero = jnp.zeros_like(qt)
        return (jnp.where(drow < HEAD_DIM, qt, zero), jnp.where(drow >= HEAD_DIM, qt, zero))

    def near_blocks(qi, n_full):
        qh = head_queries(qi)
        kbs = [2 * qi + 1, 2 * qi] + [2 * qi - 1 - j for j in range(n_full)]
        ks = [k_block(kb) for kb in kbs]
        vs = [vt_ref[0, 0, kb] for kb in kbs]
        rmin = None
        for h in range(2):
            z0, c0 = scores(ks[0], qh[h][:, TK:], mask_top)
            z1, c1 = scores(ks[1], qh[h], mask_lo)
            full = [scores(kf, qh[h], None) for kf in ks[2:]]
            a0 = jnp.where(mask_top, jnp.exp(z0 - c0), 0.0).astype(BF16)
            r = jnp.concatenate([jnp.zeros((1, TK), F32), c0[0:1, :]], axis=1)
            a1 = jnp.where(mask_lo, jnp.exp(z1 - (c1 + r)), 0.0).astype(BF16)
            r = r + c1[0:1, :]
            parts = [jnp.concatenate([jnp.zeros((TK, TK), BF16), a0], axis=1), a1]
            for z, c in full:
                parts.append(jnp.exp(z - (c + r)).astype(BF16))
                r = r + c[0:1, :]
            hs = slice(h * HEAD_DIM, (h + 1) * HEAD_DIM)
            v_cat = jnp.concatenate([v[hs, :] for v in vs], axis=1)
            acc_ref[h] = jnp.dot(v_cat, jnp.concatenate(parts, axis=0), preferred_element_type=F32)
            r_ref[h] = r
            m = jnp.min(r)
            rmin = m if rmin is None else jnp.minimum(rmin, m)
        return qh, rmin

    def far_block(kb, qh):
        kblk = k_block(kb)
        vblk = vt_ref[0, 0, kb]
        rmin = None
        for h in range(2):
            z, c = scores(kblk, qh[h], None)
            r_old = r_ref[h]
            a = jnp.exp(z - (c + r_old)).astype(BF16)
            acc_ref[h] += jnp.dot(vblk[h * HEAD_DIM:(h + 1) * HEAD_DIM, :], a,
                                  preferred_element_type=F32)
            r_new = r_old + c[0:1, :]
            r_ref[h] = r_new
            m = jnp.min(r_new)
            rmin = m if rmin is None else jnp.minimum(rmin, m)
        return rmin

    def write_out(qi):
        acc = jnp.concatenate([acc_ref[0], acc_ref[1]], axis=0)
        o_ref[0, pl.ds(pl.multiple_of(qi * TQ, TQ), TQ), :] = acc.T

    near_blocks(0, 0)
    write_out(0)

    def q_tile(qi, carry):
        qh, rmin0 = near_blocks(qi, 2)

        def cond(c):
            kb, rmin = c
            return jnp.logical_and(kb >= 0, rmin < SKIP_THRESHOLD)

        def body(c):
            kb, _ = c
            return kb - 1, far_block(kb, qh)

        lax.while_loop(cond, body, (2 * qi - 3, rmin0))
        write_out(qi)
        return carry

    lax.fori_loop(1, nq, q_tile, 0)


def _attention(qt, k, vt):
    B, _, nq, _, _ = qt.shape
    L = k.shape[1]
    nk = vt.shape[2]
    return pl.pallas_call(
        _attn_kernel,
        grid=(B, HEAD_PAIRS),
        in_specs=[
            pl.BlockSpec((1, 1, nq, LANES, TQ), lambda b, p: (b, p, 0, 0, 0)),
            pl.BlockSpec((1, L, LANES), lambda b, p: (b, 0, p)),
            pl.BlockSpec((1, 1, nk, LANES, TK), lambda b, p: (b, p, 0, 0, 0)),
        ],
        out_specs=pl.BlockSpec((1, L, LANES), lambda b, p: (b, 0, p)),
        out_shape=jax.ShapeDtypeStruct((B, L, SB_WIDTH), F32),
        scratch_shapes=[
            pltpu.VMEM((2, HEAD_DIM, TQ), F32),
            pltpu.VMEM((2, 1, TQ), F32),
        ],
        compiler_params=pltpu.CompilerParams(
            dimension_semantics=("parallel", "parallel"),
            vmem_limit_bytes=VMEM_LIMIT_BYTES),
        name="sb_attention",
    )(qt, k, vt)


def _gelu_tanh(x):
    c = math.sqrt(2.0 / math.pi)
    return 0.5 * x * (1.0 + jnp.tanh(c * (x + 0.044715 * (x * x * x))))


def _s5_kernel(u_ref, perm_ref, permt_ref, bre_ref, bim_ref, are_ref, aim_ref, cre_ref, cim_ref,
               d_ref, wglu_ref, bglu_ref, gs_ref, o_ref, bu_ref, xs_ref, st_ref, *, batch):
    @pl.when(pl.program_id(0) == 0)
    def _():
        st_ref[...] = jnp.zeros_like(st_ref)

    rows = batch * TT
    u_f32 = jnp.dot(perm_ref[...], u_ref[...].reshape(rows, SSM_WIDTH),
                    preferred_element_type=F32)
    u = u_f32.astype(BF16)
    half = SSM_REAL // 2
    for kb in range(2):
        ub = u[:, kb * MXU_DIM:(kb + 1) * MXU_DIM]
        bu_ref[:, kb * half:(kb + 1) * half] = jnp.dot(
            ub, bre_ref[kb], preferred_element_type=F32)
        bu_ref[:, SSM_REAL + kb * half:SSM_REAL + (kb + 1) * half] = jnp.dot(
            ub, bim_ref[kb], preferred_element_type=F32)

    for sc in range(SSM_REAL // SCAN_W):
        re = slice(sc * SCAN_W, (sc + 1) * SCAN_W)
        im = slice(SSM_REAL + sc * SCAN_W, SSM_REAL + (sc + 1) * SCAN_W)
        ar = jnp.broadcast_to(are_ref[:, re], (batch, SCAN_W))
        ai = jnp.broadcast_to(aim_ref[:, re], (batch, SCAN_W))

        def step(t, carry, re=re, im=im, ar=ar, ai=ai):
            xr, xi = carry
            rows = pl.ds(pl.multiple_of(t * batch, batch), batch)
            nxr = ar * xr - ai * xi + bu_ref[rows, re]
            nxi = ar * xi + ai * xr + bu_ref[rows, im]
            xs_ref[rows, re] = nxr.astype(BF16)
            xs_ref[rows, im] = nxi.astype(BF16)
            return nxr, nxi

        xr, xi = lax.fori_loop(0, TT, step, (st_ref[:, re], st_ref[:, im]), unroll=4)
        st_ref[:, re] = xr
        st_ref[:, im] = xi

    ys = []
    for ob in range(2):
        y = jnp.dot(xs_ref[:, ob * half:(ob + 1) * half], cre_ref[ob],
                    preferred_element_type=F32)
        y = y + jnp.dot(xs_ref[:, SSM_REAL + ob * half:SSM_REAL + (ob + 1) * half], cim_ref[ob],
                        preferred_element_type=F32)
        ys.append(y)
    y = jnp.concatenate(ys, axis=1) + d_ref[...] * u_f32
    y = _gelu_tanh(y)
    gate = jax.nn.sigmoid(
        jnp.dot(y.astype(BF16), wglu_ref[...], preferred_element_type=F32) + bglu_ref[...])
    out_tb = (_rms(y * gate) * gs_ref[...]).astype(BF16)
    out_bt = jnp.dot(permt_ref[...], out_tb, preferred_element_type=F32)
    o_ref[...] = out_bt.astype(o_ref.dtype).reshape(batch, TT, SSM_WIDTH)


def _s5(u, bre, bim, are, aim, cre, cim, d, wglu, bglu, gs):
    batch, L, _ = u.shape
    rows = TT * batch
    half = SSM_REAL // 2
    c2 = lambda i: (0, 0)
    c3 = lambda i: (0, 0, 0)
    src = (jnp.arange(rows) % batch) * TT + jnp.arange(rows) // batch
    perm = (src[:, None] == jnp.arange(rows)[None, :]).astype(BF16)
    return pl.pallas_call(
        functools.partial(_s5_kernel, batch=batch),
        grid=(L // TT,),
        in_specs=[
            pl.BlockSpec((batch, TT, SSM_WIDTH), lambda i: (0, i, 0)),
            pl.BlockSpec((rows, rows), c2),
            pl.BlockSpec((rows, rows), c2),
            pl.BlockSpec((2, MXU_DIM, half), c3),
            pl.BlockSpec((2, MXU_DIM, half), c3),
            pl.BlockSpec((1, SSM_REAL), c2),
            pl.BlockSpec((1, SSM_REAL), c2),
            pl.BlockSpec((2, half, MXU_DIM), c3),
            pl.BlockSpec((2, half, MXU_DIM), c3),
            pl.BlockSpec((1, SSM_WIDTH), c2),
            pl.BlockSpec((SSM_WIDTH, SSM_WIDTH), c2),
            pl.BlockSpec((1, SSM_WIDTH), c2),
            pl.BlockSpec((1, SSM_WIDTH), c2),
        ],
        out_specs=pl.BlockSpec((batch, TT, SSM_WIDTH), lambda i: (0, i, 0)),
        out_shape=jax.ShapeDtypeStruct((batch, L, SSM_WIDTH), BF16),
        scratch_shapes=[
            pltpu.VMEM((rows, 2 * SSM_REAL), F32),
            pltpu.VMEM((rows, 2 * SSM_REAL), BF16),
            pltpu.VMEM((batch, 2 * SSM_REAL), F32),
        ],
        compiler_params=pltpu.CompilerParams(
            dimension_semantics=("arbitrary",),
            vmem_limit_bytes=VMEM_LIMIT_BYTES),
        name="s5_glu",
    )(u, perm, perm.T, bre, bim, are, aim, cre, cim, d, wglu, bglu, gs)


def _out_mlp_kernel(x_ref, sb_ref, ssm_ref, ga_ref, wout_ref, g2_ref, w1_ref, w2_ref, o_ref):
    x = x_ref[0]
    an = (_rms(sb_ref[0]) * ga_ref[...]).astype(BF16)
    h = x + jnp.dot(an, wout_ref[:SB_WIDTH, :], preferred_element_type=F32)
    h = h + jnp.dot(ssm_ref[0], wout_ref[SB_WIDTH:, :], preferred_element_type=F32)
    hn = (_rms(h) * g2_ref[...]).astype(BF16)
    o_ref[0] = h
    for c in range(D_FF // FF_CHUNK):
        a = jnp.dot(hn, w1_ref[:, c * FF_CHUNK:(c + 1) * FF_CHUNK], preferred_element_type=F32)
        a = jnp.square(jnp.maximum(a, 0.0)).astype(BF16)
        o_ref[0] += jnp.dot(a, w2_ref[c * FF_CHUNK:(c + 1) * FF_CHUNK, :],
                            preferred_element_type=F32)


def _out_mlp(x, sb, ssm, ga, wout, g2, w1, w2):
    B, L, D = x.shape
    const = lambda b, t: (0, 0)
    resident = functools.partial(pl.BlockSpec, index_map=const)
    return pl.pallas_call(
        _out_mlp_kernel,
        grid=(B, L // TM_MLP),
        in_specs=[
            pl.BlockSpec((1, TM_MLP, D), lambda b, t: (b, t, 0)),
            pl.BlockSpec((1, TM_MLP, SB_WIDTH), lambda b, t: (b, t, 0)),
            pl.BlockSpec((1, TM_MLP, SSM_WIDTH), lambda b, t: (b, t, 0)),
            pl.BlockSpec((1, SB_WIDTH), const),
            resident((D, D)),
            pl.BlockSpec((1, D), const),
            resident((D, D_FF)),
            resident((D_FF, D)),
        ],
        out_specs=pl.BlockSpec((1, TM_MLP, D), lambda b, t: (b, t, 0)),
        out_shape=jax.ShapeDtypeStruct((B, L, D), F32),
        compiler_params=pltpu.CompilerParams(
            dimension_semantics=("parallel", "parallel"),
            vmem_limit_bytes=VMEM_LIMIT_BYTES),
        name="out_mlp",
    )(x, sb, ssm, ga, wout, g2, w1, w2)


def _s5_params(lam_re, lam_im, log_dt, b_re, b_im, c_re, c_im):
    G, P, H = SSM_GROUPS, SSM_STATE, SSM_GROUP
    lr, li = lam_re.astype(F32), lam_im.astype(F32)
    dt = jnp.exp(log_dt.astype(F32))[:, None]
    mag = jnp.exp(lr * dt)
    are, aim = mag * jnp.cos(li * dt), mag * jnp.sin(li * dt)
    den = lr * lr + li * li
    wr = ((are - 1.0) * lr + aim * li) / den
    wi = (aim * lr - (are - 1.0) * li) / den
    bbr = wr[:, :, None] * b_re.astype(F32) - wi[:, :, None] * b_im.astype(F32)
    bbi = wr[:, :, None] * b_im.astype(F32) + wi[:, :, None] * b_re.astype(F32)
    are, aim = are.reshape(1, G * P), aim.reshape(1, G * P)
    gpb = MXU_DIM // H
    eye = jnp.eye(gpb, dtype=F32)

    def b_layout(b):
        b = b.reshape(G // gpb, gpb, P, H)
        return jnp.einsum("kgph,gf->kghfp", b, eye).reshape(G // gpb, gpb * H, gpb * P).astype(BF16)

    def c_layout(c):
        c = c.reshape(G // gpb, gpb, H, P)
        return jnp.einsum("kghp,gf->kgpfh", c, eye).reshape(G // gpb, gpb * P, gpb * H).astype(BF16)

    return (b_layout(bbr), b_layout(bbi), are, aim,
            c_layout(c_re.astype(F32)), c_layout(-c_im.astype(F32)))


def kernel(x, norm1_g, w_in, q_norm_g, k_norm_g, ssm_lambda_re, ssm_lambda_im, ssm_log_dt,
           ssm_b_re, ssm_b_im, ssm_c_re, ssm_c_im, ssm_d, w_glu, b_glu, attn_out_g,
           ssm_out_g, w_out, norm2_g, w_mlp_in, w_mlp_out):
    B, L, D = x.shape
    assert (D, L % TM_PROJ, L % TM_MLP, L % TT) == (D_MODEL, 0, 0, 0)
    heads = SB_WIDTH // HEAD_DIM
    row = lambda g: g.astype(F32).reshape(1, -1)

    head_id = jnp.arange(SB_WIDTH) // HEAD_DIM
    hsel = (head_id[:, None] == head_id[None, :]).astype(BF16) * (1.0 / HEAD_DIM)
    gq = row(jnp.tile(q_norm_g.astype(F32), heads)) * (1.0 / math.sqrt(HEAD_DIM))
    gk = row(jnp.tile(k_norm_g.astype(F32), heads))

    qt, k, vt, u = _inproj(x.astype(F32), row(norm1_g), w_in.astype(BF16), hsel, gq, gk)
    sb = _attention(qt, k, vt)

    bre, bim, are, aim, cre, cim = _s5_params(
        ssm_lambda_re, ssm_lambda_im, ssm_log_dt, ssm_b_re, ssm_b_im, ssm_c_re, ssm_c_im)
    ssm = _s5(u, bre, bim, are, aim, cre, cim,
              row(ssm_d), w_glu.astype(BF16), row(b_glu), row(ssm_out_g))

    out = _out_mlp(x.astype(F32), sb, ssm, row(attn_out_g),
                   w_out.astype(BF16), row(norm2_g), w_mlp_in.astype(BF16), w_mlp_out.astype(BF16))
    return out.astype(x.dtype)
```

```python
import functools
import math

import jax
import jax.numpy as jnp
from jax import lax
from jax.experimental import pallas as pl
from jax.experimental.pallas import tpu as pltpu

F32 = jnp.float32
BF16 = jnp.bfloat16

D_MODEL = 1024
SB_WIDTH = 512
HEAD_DIM = 64
HEAD_PAIRS = SB_WIDTH // (2 * HEAD_DIM)
SSM_WIDTH = 512
SSM_GROUP = 16
SSM_GROUPS = 32
SSM_STATE = 64
SSM_REAL = SSM_GROUPS * SSM_STATE
D_FF = 4 * D_MODEL
EPS = 1e-6

LANES = 128
MXU_DIM = 256
VMEM_LIMIT_BYTES = 56 * 1024 * 1024

TM_PROJ = 512
TQ = 256
TK = 128
TT = 32
SCAN_W = 512
TM_MLP = 512
FF_CHUNK = 1024

SKIP_THRESHOLD = 88.0
N_FAR = 1
NO_KEYS = 1e30


def _rms(x):
    return x * lax.rsqrt(jnp.mean(x * x, axis=-1, keepdims=True) + EPS)


def _inproj_kernel(x_ref, g1_ref, win_ref, hsel_ref, gq_ref, gk_ref,
                   qt_ref, k_ref, vt_ref, u_ref):
    x = x_ref[0]
    xn = (_rms(x) * g1_ref[...]).astype(BF16)
    proj = jnp.dot(xn, win_ref[...], preferred_element_type=F32)
    q = proj[:, 0 * SB_WIDTH:1 * SB_WIDTH]
    k = proj[:, 1 * SB_WIDTH:2 * SB_WIDTH]
    v = proj[:, 2 * SB_WIDTH:3 * SB_WIDTH]
    u = proj[:, 3 * SB_WIDTH:]
    hsel = hsel_ref[...]
    msq = jnp.dot((q * q).astype(BF16), hsel, preferred_element_type=F32)
    msk = jnp.dot((k * k).astype(BF16), hsel, preferred_element_type=F32)
    qn = q * lax.rsqrt(msq + EPS) * gq_ref[...]
    kn = k * lax.rsqrt(msk + EPS) * gk_ref[...]
    k_ref[0] = kn.astype(BF16)
    u_ref[0] = u.astype(BF16)
    for j in range(TM_PROJ // TQ):
        qt = qn[j * TQ:(j + 1) * TQ, :].T
        qt_ref[0, :, j] = qt.astype(BF16).reshape(HEAD_PAIRS, LANES, TQ)
    for j in range(TM_PROJ // TK):
        vt = v[j * TK:(j + 1) * TK, :].T
        vt_ref[0, :, j] = vt.astype(BF16).reshape(HEAD_PAIRS, LANES, TK)


def _inproj(x, g1, win, hsel, gq, gk):
    B, L, D = x.shape
    nq, nk = L // TQ, L // TK
    const = lambda b, t: (0, 0)
    return pl.pallas_call(
        _inproj_kernel,
        grid=(B, L // TM_PROJ),
        in_specs=[
            pl.BlockSpec((1, TM_PROJ, D), lambda b, t: (b, t, 0)),
            pl.BlockSpec((1, D), const),
            pl.BlockSpec((D, 4 * SB_WIDTH), const),
            pl.BlockSpec((SB_WIDTH, SB_WIDTH), const),
            pl.BlockSpec((1, SB_WIDTH), const),
            pl.BlockSpec((1, SB_WIDTH), const),
        ],
        out_specs=[
            pl.BlockSpec((1, HEAD_PAIRS, TM_PROJ // TQ, LANES, TQ), lambda b, t: (b, 0, t, 0, 0)),
            pl.BlockSpec((1, TM_PROJ, SB_WIDTH), lambda b, t: (b, t, 0)),
            pl.BlockSpec((1, HEAD_PAIRS, TM_PROJ // TK, LANES, TK), lambda b, t: (b, 0, t, 0, 0)),
            pl.BlockSpec((1, TM_PROJ, SSM_WIDTH), lambda b, t: (b, t, 0)),
        ],
        out_shape=[
            jax.ShapeDtypeStruct((B, HEAD_PAIRS, nq, LANES, TQ), BF16),
            jax.ShapeDtypeStruct((B, L, SB_WIDTH), BF16),
            jax.ShapeDtypeStruct((B, HEAD_PAIRS, nk, LANES, TK), BF16),
            jax.ShapeDtypeStruct((B, L, SSM_WIDTH), BF16),
        ],
        compiler_params=pltpu.CompilerParams(
            dimension_semantics=("parallel", "parallel"),
            vmem_limit_bytes=VMEM_LIMIT_BYTES),
        name="inproj",
    )(x, g1, win, hsel, gq, gk)


def _attn_kernel(qt_ref, k_ref, vt_ref, o_ref, mlo_ref, z0_ref, sp0_ref, z_ref, sp_ref,
                 acc_ref, r_ref):
    nq = qt_ref.shape[2]
    row = lax.broadcasted_iota(jnp.int32, (TK, TQ), 0)
    col = lax.broadcasted_iota(jnp.int32, (TK, TQ), 1)
    mlo_ref[...] = jnp.where(row < col, 1.0, 0.0)
    srow = lax.broadcasted_iota(jnp.int32, (TK, TK), 0)
    scol = lax.broadcasted_iota(jnp.int32, (TK, TK), 1)
    tri = jnp.where(scol >= srow, 1.0, 0.0).astype(BF16)
    drow = lax.broadcasted_iota(jnp.int32, (LANES, TQ), 0)

    def weights(z, s, mask):
        return (jnp.exp(jnp.minimum(z - s, 0.0)) * mask).astype(BF16)

    def k_block(kb):
        return k_ref[0, pl.ds(pl.multiple_of(kb * TK, TK), TK), :]

    def softplus(z):
        return jnp.maximum(z, 0.0) + jnp.log(1.0 + jnp.exp(-jnp.abs(z)))

    def cumsum(sp):
        return jnp.dot(tri, sp, preferred_element_type=F32)

    def head_queries(qi):
        qt = qt_ref[0, 0, qi]
        zero = jnp.zeros_like(qt)
        return (jnp.where(drow < HEAD_DIM, qt, zero), jnp.where(drow >= HEAD_DIM, qt, zero))

    def far_kb(qi, j):
        return jnp.maximum(2 * qi - 1 - j, 0)

    def stage_a(qi, slot):
        qh = head_queries(qi)
        k_top, k_lo = k_block(2 * qi + 1), k_block(2 * qi)
        k_far = [k_block(far_kb(qi, j)) for j in range(N_FAR)]
        for h in range(2):
            z0 = jnp.dot(k_top, qh[h][:, TK:], preferred_element_type=F32)
            z0_ref[slot, h] = z0
            sp0_ref[slot, h] = (softplus(z0) * mlo_ref[:, :TK]).astype(BF16)
            z1 = jnp.dot(k_lo, qh[h], preferred_element_type=F32)
            z_ref[slot, h, 0] = z1
            sp_ref[slot, h, 0] = (softplus(z1) * mlo_ref[...]).astype(BF16)
            for j in range(N_FAR):
                z = jnp.dot(k_far[j], qh[h], preferred_element_type=F32)
                z_ref[slot, h, 1 + j] = z
                sp_ref[slot, h, 1 + j] = softplus(z).astype(BF16)

    def stage_b(qi, slot):
        kbs = [2 * qi + 1, 2 * qi] + [far_kb(qi, j) for j in range(N_FAR)]
        vs = [vt_ref[0, 0, kb] for kb in kbs]
        no_far = jnp.where(qi == 0, NO_KEYS, 0.0).astype(F32)
        rmin = None
        for h in range(2):
            c0 = cumsum(sp0_ref[slot, h])
            a0 = weights(z0_ref[slot, h], c0, mlo_ref[:, :TK])
            r = jnp.concatenate([jnp.zeros((1, TK), F32), c0[0:1, :]], axis=1)
            c1 = cumsum(sp_ref[slot, h, 0])
            a1 = weights(z_ref[slot, h, 0], c1 + r, mlo_ref[...])
            r = r + c1[0:1, :] + no_far
            parts = [jnp.concatenate([jnp.zeros((TK, TK), BF16), a0], axis=1), a1]
            for j in range(N_FAR):
                c = cumsum(sp_ref[slot, h, 1 + j])
                parts.append(jnp.exp(z_ref[slot, h, 1 + j] - (c + r)).astype(BF16))
                r = r + c[0:1, :]
            hs = slice(h * HEAD_DIM, (h + 1) * HEAD_DIM)
            v_cat = jnp.concatenate([v[hs, :] for v in vs], axis=1)
            acc_ref[h] = jnp.dot(v_cat, jnp.concatenate(parts, axis=0), preferred_element_type=F32)
            r_ref[h] = r
            m = jnp.min(r)
            rmin = m if rmin is None else jnp.minimum(rmin, m)
        return rmin

    def far_block(kb, qh):
        kblk = k_block(kb)
        vblk = vt_ref[0, 0, kb]
        rmin = None
        for h in range(2):
            z = jnp.dot(kblk, qh[h], preferred_element_type=F32)
            c = cumsum(softplus(z).astype(BF16))
            r_old = r_ref[h]
            a = jnp.exp(z - (c + r_old)).astype(BF16)
            acc_ref[h] += jnp.dot(vblk[h * HEAD_DIM:(h + 1) * HEAD_DIM, :], a,
                                  preferred_element_type=F32)
            r_new = r_old + c[0:1, :]
            r_ref[h] = r_new
            m = jnp.min(r_new)
            rmin = m if rmin is None else jnp.minimum(rmin, m)
        return rmin

    def finish(qi, rmin0):
        qh = head_queries(qi)

        def cond(c):
            kb, rmin = c
            return jnp.logical_and(kb >= 0, rmin < SKIP_THRESHOLD)

        def body(c):
            kb, _ = c
            return kb - 1, far_block(kb, qh)

        lax.while_loop(cond, body, (2 * qi - 1 - N_FAR, rmin0))
        acc = jnp.concatenate([acc_ref[0], acc_ref[1]], axis=0)
        o_ref[0, pl.ds(pl.multiple_of(qi * TQ, TQ), TQ), :] = acc.T

    stage_a(0, 0)

    def tile_pair(m, carry):
        even = 2 * m
        stage_a(even + 1, 1)
        finish(even, stage_b(even, 0))
        stage_a(even + 2, 0)
        finish(even + 1, stage_b(even + 1, 1))
        return carry

    lax.fori_loop(0, (nq - 2) // 2, tile_pair, 0)
    stage_a(nq - 1, 1)
    finish(nq - 2, stage_b(nq - 2, 0))
    finish(nq - 1, stage_b(nq - 1, 1))


def _attention(qt, k, vt):
    B, _, nq, _, _ = qt.shape
    L = k.shape[1]
    nk = vt.shape[2]
    assert nq % 2 == 0 and nq >= 2
    return pl.pallas_call(
        _attn_kernel,
        grid=(B, HEAD_PAIRS),
        in_specs=[
            pl.BlockSpec((1, 1, nq, LANES, TQ), lambda b, p: (b, p, 0, 0, 0)),
            pl.BlockSpec((1, L, LANES), lambda b, p: (b, 0, p)),
            pl.BlockSpec((1, 1, nk, LANES, TK), lambda b, p: (b, p, 0, 0, 0)),
        ],
        out_specs=pl.BlockSpec((1, L, LANES), lambda b, p: (b, 0, p)),
        out_shape=jax.ShapeDtypeStruct((B, L, SB_WIDTH), F32),
        scratch_shapes=[
            pltpu.VMEM((TK, TQ), F32),
            pltpu.VMEM((2, 2, TK, TK), F32),
            pltpu.VMEM((2, 2, TK, TK), BF16),
            pltpu.VMEM((2, 2, 1 + N_FAR, TK, TQ), F32),
            pltpu.VMEM((2, 2, 1 + N_FAR, TK, TQ), BF16),
            pltpu.VMEM((2, HEAD_DIM, TQ), F32),
            pltpu.VMEM((2, 1, TQ), F32),
        ],
        compiler_params=pltpu.CompilerParams(
            dimension_semantics=("parallel", "parallel"),
            vmem_limit_bytes=VMEM_LIMIT_BYTES),
        name="sb_attention",
    )(qt, k, vt)


def _gelu_tanh(x):
    c = math.sqrt(2.0 / math.pi)
    return 0.5 * x * (1.0 + jnp.tanh(c * (x + 0.044715 * (x * x * x))))


def _s5_kernel(u_ref, perm_ref, permt_ref, bre_ref, bim_ref, are_ref, aim_ref, cre_ref, cim_ref,
               d_ref, wglu_ref, bglu_ref, gs_ref, o_ref, ub_ref, bu_ref, xs_ref, st_ref, *, batch):
    s = pl.program_id(0)

    @pl.when(s == 0)
    def _():
        for ref in (ub_ref, bu_ref, xs_ref, st_ref):
            ref[...] = jnp.zeros_like(ref)

    rows = batch * TT
    half = SSM_REAL // 2
    cur = lax.rem(s, 2)
    prev = 1 - cur

    u = jnp.dot(perm_ref[...], u_ref[...].reshape(rows, SSM_WIDTH),
                preferred_element_type=F32).astype(BF16)
    ub_ref[lax.rem(s, 3)] = u
    for kb in range(2):
        ub = u[:, kb * MXU_DIM:(kb + 1) * MXU_DIM]
        bu_ref[cur, :, kb * half:(kb + 1) * half] = jnp.dot(
            ub, bre_ref[kb], preferred_element_type=F32)
        bu_ref[cur, :, SSM_REAL + kb * half:SSM_REAL + (kb + 1) * half] = jnp.dot(
            ub, bim_ref[kb], preferred_element_type=F32)

    ys = []
    for ob in range(2):
        y = jnp.dot(xs_ref[cur, :, ob * half:(ob + 1) * half], cre_ref[ob],
                    preferred_element_type=F32)
        y = y + jnp.dot(xs_ref[cur, :, SSM_REAL + ob * half:SSM_REAL + (ob + 1) * half],
                        cim_ref[ob], preferred_element_type=F32)
        ys.append(y)
    y = jnp.concatenate(ys, axis=1) + d_ref[...] * ub_ref[lax.rem(s + 1, 3)].astype(F32)
    y = _gelu_tanh(y)
    gate = jax.nn.sigmoid(
        jnp.dot(y.astype(BF16), wglu_ref[...], preferred_element_type=F32) + bglu_ref[...])
    out_tb = (_rms(y * gate) * gs_ref[...]).astype(BF16)
    out_bt = jnp.dot(permt_ref[...], out_tb, preferred_element_type=F32)
    o_ref[...] = out_bt.astype(o_ref.dtype).reshape(batch, TT, SSM_WIDTH)

    for sc in range(SSM_REAL // SCAN_W):
        re = slice(sc * SCAN_W, (sc + 1) * SCAN_W)
        im = slice(SSM_REAL + sc * SCAN_W, SSM_REAL + (sc + 1) * SCAN_W)
        ar = jnp.broadcast_to(are_ref[:, re], (batch, SCAN_W))
        ai = jnp.broadcast_to(aim_ref[:, re], (batch, SCAN_W))

        def step(t, carry, re=re, im=im, ar=ar, ai=ai):
            xr, xi = carry
            rs = pl.ds(pl.multiple_of(t * batch, batch), batch)
            nxr = ar * xr - ai * xi + bu_ref[prev, rs, re]
            nxi = ar * xi + ai * xr + bu_ref[prev, rs, im]
            xs_ref[prev, rs, re] = nxr.astype(BF16)
            xs_ref[prev, rs, im] = nxi.astype(BF16)
            return nxr, nxi

        xr, xi = lax.fori_loop(0, TT, step, (st_ref[:, re], st_ref[:, im]), unroll=4)
        st_ref[:, re] = xr
        st_ref[:, im] = xi


def _s5(u, bre, bim, are, aim, cre, cim, d, wglu, bglu, gs):
    batch, L, _ = u.shape
    rows = TT * batch
    tiles = L // TT
    half = SSM_REAL // 2
    c2 = lambda s: (0, 0)
    c3 = lambda s: (0, 0, 0)
    src = (jnp.arange(rows) % batch) * TT + jnp.arange(rows) // batch
    perm = (src[:, None] == jnp.arange(rows)[None, :]).astype(BF16)
    return pl.pallas_call(
        functools.partial(_s5_kernel, batch=batch),
        grid=(tiles + 2,),
        in_specs=[
            pl.BlockSpec((batch, TT, SSM_WIDTH), lambda s: (0, jnp.minimum(s, tiles - 1), 0)),
            pl.BlockSpec((rows, rows), c2),
            pl.BlockSpec((rows, rows), c2),
            pl.BlockSpec((2, MXU_DIM, half), c3),
            pl.BlockSpec((2, MXU_DIM, half), c3),
            pl.BlockSpec((1, SSM_REAL), c2),
            pl.BlockSpec((1, SSM_REAL), c2),
            pl.BlockSpec((2, half, MXU_DIM), c3),
            pl.BlockSpec((2, half, MXU_DIM), c3),
            pl.BlockSpec((1, SSM_WIDTH), c2),
            pl.BlockSpec((SSM_WIDTH, SSM_WIDTH), c2),
            pl.BlockSpec((1, SSM_WIDTH), c2),
            pl.BlockSpec((1, SSM_WIDTH), c2),
        ],
        out_specs=pl.BlockSpec((batch, TT, SSM_WIDTH), lambda s: (0, jnp.maximum(s - 2, 0), 0)),
        out_shape=jax.ShapeDtypeStruct((batch, L, SSM_WIDTH), BF16),
        scratch_shapes=[
            pltpu.VMEM((3, rows, SSM_WIDTH), BF16),
            pltpu.VMEM((2, rows, 2 * SSM_REAL), F32),
            pltpu.VMEM((2, rows, 2 * SSM_REAL), BF16),
            pltpu.VMEM((batch, 2 * SSM_REAL), F32),
        ],
        compiler_params=pltpu.CompilerParams(
            dimension_semantics=("arbitrary",),
            vmem_limit_bytes=VMEM_LIMIT_BYTES),
        name="s5_glu",
    )(u, perm, perm.T, bre, bim, are, aim, cre, cim, d, wglu, bglu, gs)


def _out_mlp_kernel(x_ref, sb_ref, ssm_ref, ga_ref, wout_ref, g2_ref, w1_ref, w2_ref, o_ref):
    x = x_ref[0]
    an = (_rms(sb_ref[0]) * ga_ref[...]).astype(BF16)
    h = x + jnp.dot(an, wout_ref[:SB_WIDTH, :], preferred_element_type=F32)
    h = h + jnp.dot(ssm_ref[0], wout_ref[SB_WIDTH:, :], preferred_element_type=F32)
    hn = (_rms(h) * g2_ref[...]).astype(BF16)
    o_ref[0] = h
    for c in range(D_FF // FF_CHUNK):
        a = jnp.dot(hn, w1_ref[:, c * FF_CHUNK:(c + 1) * FF_CHUNK], preferred_element_type=F32)
        a = jnp.square(jnp.maximum(a, 0.0)).astype(BF16)
        o_ref[0] += jnp.dot(a, w2_ref[c * FF_CHUNK:(c + 1) * FF_CHUNK, :],
                            preferred_element_type=F32)


def _out_mlp(x, sb, ssm, ga, wout, g2, w1, w2):
    B, L, D = x.shape
    const = lambda b, t: (0, 0)
    resident = functools.partial(pl.BlockSpec, index_map=const)
    return pl.pallas_call(
        _out_mlp_kernel,
        grid=(B, L // TM_MLP),
        in_specs=[
            pl.BlockSpec((1, TM_MLP, D), lambda b, t: (b, t, 0)),
            pl.BlockSpec((1, TM_MLP, SB_WIDTH), lambda b, t: (b, t, 0)),
            pl.BlockSpec((1, TM_MLP, SSM_WIDTH), lambda b, t: (b, t, 0)),
            pl.BlockSpec((1, SB_WIDTH), const),
            resident((D, D)),
            pl.BlockSpec((1, D), const),
            resident((D, D_FF)),
            resident((D_FF, D)),
        ],
        out_specs=pl.BlockSpec((1, TM_MLP, D), lambda b, t: (b, t, 0)),
        out_shape=jax.ShapeDtypeStruct((B, L, D), F32),
        compiler_params=pltpu.CompilerParams(
            dimension_semantics=("parallel", "parallel"),
            vmem_limit_bytes=VMEM_LIMIT_BYTES),
        name="out_mlp",
    )(x, sb, ssm, ga, wout, g2, w1, w2)


def _s5_params(lam_re, lam_im, log_dt, b_re, b_im, c_re, c_im):
    G, P, H = SSM_GROUPS, SSM_STATE, SSM_GROUP
    lr, li = lam_re.astype(F32), lam_im.astype(F32)
    dt = jnp.exp(log_dt.astype(F32))[:, None]
    mag = jnp.exp(lr * dt)
    are, aim = mag * jnp.cos(li * dt), mag * jnp.sin(li * dt)
    den = lr * lr + li * li
    wr = ((are - 1.0) * lr + aim * li) / den
    wi = (aim * lr - (are - 1.0) * li) / den
    bbr = wr[:, :, None] * b_re.astype(F32) - wi[:, :, None] * b_im.astype(F32)
    bbi = wr[:, :, None] * b_im.astype(F32) + wi[:, :, None] * b_re.astype(F32)
    are, aim = are.reshape(1, G * P), aim.reshape(1, G * P)
    gpb = MXU_DIM // H
    eye = jnp.eye(gpb, dtype=F32)

    def b_layout(b):
        b = b.reshape(G // gpb, gpb, P, H)
        return jnp.einsum("kgph,gf->kghfp", b, eye).reshape(G // gpb, gpb * H, gpb * P).astype(BF16)

    def c_layout(c):
        c = c.reshape(G // gpb, gpb, H, P)
        return jnp.einsum("kghp,gf->kgpfh", c, eye).reshape(G // gpb, gpb * P, gpb * H).astype(BF16)

    return (b_layout(bbr), b_layout(bbi), are, aim,
            c_layout(c_re.astype(F32)), c_layout(-c_im.astype(F32)))


def kernel(x, norm1_g, w_in, q_norm_g, k_norm_g, ssm_lambda_re, ssm_lambda_im, ssm_log_dt,
           ssm_b_re, ssm_b_im, ssm_c_re, ssm_c_im, ssm_d, w_glu, b_glu, attn_out_g,
           ssm_out_g, w_out, norm2_g, w_mlp_in, w_mlp_out):
    B, L, D = x.shape
    assert (D, L % TM_PROJ, L % TM_MLP, L % TT) == (D_MODEL, 0, 0, 0)
    heads = SB_WIDTH // HEAD_DIM
    row = lambda g: g.astype(F32).reshape(1, -1)

    head_id = jnp.arange(SB_WIDTH) // HEAD_DIM
    hsel = (head_id[:, None] == head_id[None, :]).astype(BF16) * (1.0 / HEAD_DIM)
    gq = row(jnp.tile(q_norm_g.astype(F32), heads)) * (1.0 / math.sqrt(HEAD_DIM))
    gk = row(jnp.tile(k_norm_g.astype(F32), heads))

    qt, k, vt, u = _inproj(x.astype(F32), row(norm1_g), w_in.astype(BF16), hsel, gq, gk)
    sb = _attention(qt, k, vt)

    bre, bim, are, aim, cre, cim = _s5_params(
        ssm_lambda_re, ssm_lambda_im, ssm_log_dt, ssm_b_re, ssm_b_im, ssm_c_re, ssm_c_im)
    ssm = _s5(u, bre, bim, are, aim, cre, cim,
              row(ssm_d), w_glu.astype(BF16), row(b_glu), row(ssm_out_g))

    out = _out_mlp(x.astype(F32), sb, ssm, row(attn_out_g),
                   w_out.astype(BF16), row(norm2_g), w_mlp_in.astype(BF16), w_mlp_out.astype(BF16))
    return out.astype(x.dtype)
```

```python
import functools
import math

import jax
import jax.numpy as jnp
from jax import lax
from jax.experimental import pallas as pl
from jax.experimental.pallas import tpu as pltpu

F32 = jnp.float32
BF16 = jnp.bfloat16

D_MODEL = 1024
SB_WIDTH = 512
HEAD_DIM = 64
HEAD_PAIRS = SB_WIDTH // (2 * HEAD_DIM)
SSM_WIDTH = 512
SSM_GROUP = 16
SSM_GROUPS = 32
SSM_STATE = 64
SSM_REAL = SSM_GROUPS * SSM_STATE
D_FF = 4 * D_MODEL
EPS = 1e-6

LANES = 128
MXU_DIM = 256
VMEM_LIMIT_BYTES = 56 * 1024 * 1024

TM_PROJ = 512
TQ = 256
TK = 128
TT = 32
SCAN_W = 512
TM_MLP = 512
FF_SUB = 512

SKIP_THRESHOLD = 88.0
N_FAR = 1
NO_KEYS = 1e30


def _rms(x):
    return x * lax.rsqrt(jnp.mean(x * x, axis=-1, keepdims=True) + EPS)


def _inproj_kernel(x_ref, g1_ref, win_ref, hsel_ref, gq_ref, gk_ref,
                   qt_ref, k_ref, vt_ref, u_ref):
    x = x_ref[0]
    xn = (_rms(x) * g1_ref[...]).astype(BF16)
    proj = jnp.dot(xn, win_ref[...], preferred_element_type=F32)
    q = proj[:, 0 * SB_WIDTH:1 * SB_WIDTH]
    k = proj[:, 1 * SB_WIDTH:2 * SB_WIDTH]
    v = proj[:, 2 * SB_WIDTH:3 * SB_WIDTH]
    u = proj[:, 3 * SB_WIDTH:]
    hsel = hsel_ref[...]
    msq = jnp.dot((q * q).astype(BF16), hsel, preferred_element_type=F32)
    msk = jnp.dot((k * k).astype(BF16), hsel, preferred_element_type=F32)
    qn = q * lax.rsqrt(msq + EPS) * gq_ref[...]
    kn = k * lax.rsqrt(msk + EPS) * gk_ref[...]
    k_ref[0] = kn.astype(BF16)
    u_ref[0] = u.astype(BF16)
    for j in range(TM_PROJ // TQ):
        qt = qn[j * TQ:(j + 1) * TQ, :].T
        qt_ref[0, :, j] = qt.astype(BF16).reshape(HEAD_PAIRS, LANES, TQ)
    for j in range(TM_PROJ // TK):
        vt = v[j * TK:(j + 1) * TK, :].T
        vt_ref[0, :, j] = vt.astype(BF16).reshape(HEAD_PAIRS, LANES, TK)


def _inproj(x, g1, win, hsel, gq, gk):
    B, L, D = x.shape
    nq, nk = L // TQ, L // TK
    const = lambda b, t: (0, 0)
    return pl.pallas_call(
        _inproj_kernel,
        grid=(B, L // TM_PROJ),
        in_specs=[
            pl.BlockSpec((1, TM_PROJ, D), lambda b, t: (b, t, 0)),
            pl.BlockSpec((1, D), const),
            pl.BlockSpec((D, 4 * SB_WIDTH), const),
            pl.BlockSpec((SB_WIDTH, SB_WIDTH), const),
            pl.BlockSpec((1, SB_WIDTH), const),
            pl.BlockSpec((1, SB_WIDTH), const),
        ],
        out_specs=[
            pl.BlockSpec((1, HEAD_PAIRS, TM_PROJ // TQ, LANES, TQ), lambda b, t: (b, 0, t, 0, 0)),
            pl.BlockSpec((1, TM_PROJ, SB_WIDTH), lambda b, t: (b, t, 0)),
            pl.BlockSpec((1, HEAD_PAIRS, TM_PROJ // TK, LANES, TK), lambda b, t: (b, 0, t, 0, 0)),
            pl.BlockSpec((1, TM_PROJ, SSM_WIDTH), lambda b, t: (b, t, 0)),
        ],
        out_shape=[
            jax.ShapeDtypeStruct((B, HEAD_PAIRS, nq, LANES, TQ), BF16),
            jax.ShapeDtypeStruct((B, L, SB_WIDTH), BF16),
            jax.ShapeDtypeStruct((B, HEAD_PAIRS, nk, LANES, TK), BF16),
            jax.ShapeDtypeStruct((B, L, SSM_WIDTH), BF16),
        ],
        compiler_params=pltpu.CompilerParams(
            dimension_semantics=("parallel", "parallel"),
            vmem_limit_bytes=VMEM_LIMIT_BYTES),
        name="inproj",
    )(x, g1, win, hsel, gq, gk)


def _gelu_tanh(x):
    c = math.sqrt(2.0 / math.pi)
    return 0.5 * x * (1.0 + jnp.tanh(c * (x + 0.044715 * (x * x * x))))


def _s5_kernel(u_ref, perm_ref, permt_ref, bre_ref, bim_ref, are_ref, aim_ref, cre_ref, cim_ref,
               d_ref, wglu_ref, bglu_ref, gs_ref, o_ref, ub_ref, bu_ref, xs_ref, st_ref, *, batch):
    s = pl.program_id(0)

    @pl.when(s == 0)
    def _():
        for ref in (ub_ref, bu_ref, xs_ref, st_ref):
            ref[...] = jnp.zeros_like(ref)

    rows = batch * TT
    half = SSM_REAL // 2
    cur = lax.rem(s, 2)
    prev = 1 - cur

    u = jnp.dot(perm_ref[...], u_ref[...].reshape(rows, SSM_WIDTH),
                preferred_element_type=F32).astype(BF16)
    ub_ref[lax.rem(s, 3)] = u
    for kb in range(2):
        ub = u[:, kb * MXU_DIM:(kb + 1) * MXU_DIM]
        bu_ref[cur, :, kb * half:(kb + 1) * half] = jnp.dot(
            ub, bre_ref[kb], preferred_element_type=F32)
        bu_ref[cur, :, SSM_REAL + kb * half:SSM_REAL + (kb + 1) * half] = jnp.dot(
            ub, bim_ref[kb], preferred_element_type=F32)

    ys = []
    for ob in range(2):
        y = jnp.dot(xs_ref[cur, :, ob * half:(ob + 1) * half], cre_ref[ob],
                    preferred_element_type=F32)
        y = y + jnp.dot(xs_ref[cur, :, SSM_REAL + ob * half:SSM_REAL + (ob + 1) * half],
                        cim_ref[ob], preferred_element_type=F32)
        ys.append(y)
    y = jnp.concatenate(ys, axis=1) + d_ref[...] * ub_ref[lax.rem(s + 1, 3)].astype(F32)
    y = _gelu_tanh(y)
    gate = jax.nn.sigmoid(
        jnp.dot(y.astype(BF16), wglu_ref[...], preferred_element_type=F32) + bglu_ref[...])
    out_tb = (_rms(y * gate) * gs_ref[...]).astype(BF16)
    out_bt = jnp.dot(permt_ref[...], out_tb, preferred_element_type=F32)
    o_ref[...] = out_bt.astype(o_ref.dtype).reshape(batch, TT, SSM_WIDTH)

    for sc in range(SSM_REAL // SCAN_W):
        re = slice(sc * SCAN_W, (sc + 1) * SCAN_W)
        im = slice(SSM_REAL + sc * SCAN_W, SSM_REAL + (sc + 1) * SCAN_W)
        ar = jnp.broadcast_to(are_ref[:, re], (batch, SCAN_W))
        ai = jnp.broadcast_to(aim_ref[:, re], (batch, SCAN_W))

        def step(t, carry, re=re, im=im, ar=ar, ai=ai):
            xr, xi = carry
            rs = pl.ds(pl.multiple_of(t * batch, batch), batch)
            nxr = ar * xr - ai * xi + bu_ref[prev, rs, re]
            nxi = ar * xi + ai * xr + bu_ref[prev, rs, im]
            xs_ref[prev, rs, re] = nxr.astype(BF16)
            xs_ref[prev, rs, im] = nxi.astype(BF16)
            return nxr, nxi

        xr, xi = lax.fori_loop(0, TT, step, (st_ref[:, re], st_ref[:, im]), unroll=4)
        st_ref[:, re] = xr
        st_ref[:, im] = xi


def _s5(u, bre, bim, are, aim, cre, cim, d, wglu, bglu, gs):
    batch, L, _ = u.shape
    rows = TT * batch
    tiles = L // TT
    half = SSM_REAL // 2
    c2 = lambda s: (0, 0)
    c3 = lambda s: (0, 0, 0)
    src = (jnp.arange(rows) % batch) * TT + jnp.arange(rows) // batch
    perm = (src[:, None] == jnp.arange(rows)[None, :]).astype(BF16)
    return pl.pallas_call(
        functools.partial(_s5_kernel, batch=batch),
        grid=(tiles + 2,),
        in_specs=[
            pl.BlockSpec((batch, TT, SSM_WIDTH), lambda s: (0, jnp.minimum(s, tiles - 1), 0)),
            pl.BlockSpec((rows, rows), c2),
            pl.BlockSpec((rows, rows), c2),
            pl.BlockSpec((2, MXU_DIM, half), c3),
            pl.BlockSpec((2, MXU_DIM, half), c3),
            pl.BlockSpec((1, SSM_REAL), c2),
            pl.BlockSpec((1, SSM_REAL), c2),
            pl.BlockSpec((2, half, MXU_DIM), c3),
            pl.BlockSpec((2, half, MXU_DIM), c3),
            pl.BlockSpec((1, SSM_WIDTH), c2),
            pl.BlockSpec((SSM_WIDTH, SSM_WIDTH), c2),
            pl.BlockSpec((1, SSM_WIDTH), c2),
            pl.BlockSpec((1, SSM_WIDTH), c2),
        ],
        out_specs=pl.BlockSpec((batch, TT, SSM_WIDTH), lambda s: (0, jnp.maximum(s - 2, 0), 0)),
        out_shape=jax.ShapeDtypeStruct((batch, L, SSM_WIDTH), BF16),
        scratch_shapes=[
            pltpu.VMEM((3, rows, SSM_WIDTH), BF16),
            pltpu.VMEM((2, rows, 2 * SSM_REAL), F32),
            pltpu.VMEM((2, rows, 2 * SSM_REAL), BF16),
            pltpu.VMEM((batch, 2 * SSM_REAL), F32),
        ],
        compiler_params=pltpu.CompilerParams(
            dimension_semantics=("arbitrary",),
            vmem_limit_bytes=VMEM_LIMIT_BYTES),
        name="s5_glu",
    )(u, perm, perm.T, bre, bim, are, aim, cre, cim, d, wglu, bglu, gs)


def _attention_stages(qt_ref, k_ref, vt_ref, mlo_ref, z0_ref, sp0_ref, z_ref, sp_ref, acc_ref,
                      r_ref, write_tile):
    srow = lax.broadcasted_iota(jnp.int32, (TK, TK), 0)
    scol = lax.broadcasted_iota(jnp.int32, (TK, TK), 1)
    tri = jnp.where(scol >= srow, 1.0, 0.0).astype(BF16)
    drow = lax.broadcasted_iota(jnp.int32, (LANES, TQ), 0)

    def init():
        row = lax.broadcasted_iota(jnp.int32, (TK, TQ), 0)
        col = lax.broadcasted_iota(jnp.int32, (TK, TQ), 1)
        mlo_ref[...] = jnp.where(row < col, 1.0, 0.0)

    def weights(z, s, mask):
        return (jnp.exp(jnp.minimum(z - s, 0.0)) * mask).astype(BF16)

    def k_block(kb):
        return k_ref[0, pl.ds(pl.multiple_of(kb * TK, TK), TK), :]

    def softplus(z):
        return jnp.maximum(z, 0.0) + jnp.log(1.0 + jnp.exp(-jnp.abs(z)))

    def cumsum(sp):
        return jnp.dot(tri, sp, preferred_element_type=F32)

    def head_queries(qi):
        qt = qt_ref[0, 0, qi]
        zero = jnp.zeros_like(qt)
        return (jnp.where(drow < HEAD_DIM, qt, zero), jnp.where(drow >= HEAD_DIM, qt, zero))

    def far_kb(qi, j):
        return jnp.maximum(2 * qi - 1 - j, 0)

    def stage_a(qi, slot):
        qh = head_queries(qi)
        k_top, k_lo = k_block(2 * qi + 1), k_block(2 * qi)
        k_far = [k_block(far_kb(qi, j)) for j in range(N_FAR)]
        for h in range(2):
            z0 = jnp.dot(k_top, qh[h][:, TK:], preferred_element_type=F32)
            z0_ref[slot, h] = z0
            sp0_ref[slot, h] = (softplus(z0) * mlo_ref[:, :TK]).astype(BF16)
            z1 = jnp.dot(k_lo, qh[h], preferred_element_type=F32)
            z_ref[slot, h, 0] = z1
            sp_ref[slot, h, 0] = (softplus(z1) * mlo_ref[...]).astype(BF16)
            for j in range(N_FAR):
                z = jnp.dot(k_far[j], qh[h], preferred_element_type=F32)
                z_ref[slot, h, 1 + j] = z
                sp_ref[slot, h, 1 + j] = softplus(z).astype(BF16)

    def stage_b1(slot):
        return [(cumsum(sp0_ref[slot, h]), [cumsum(sp_ref[slot, h, j]) for j in range(1 + N_FAR)])
                for h in range(2)]

    def stage_b2(qi, slot, sums):
        kbs = [2 * qi + 1, 2 * qi] + [far_kb(qi, j) for j in range(N_FAR)]
        vs = [vt_ref[0, 0, kb] for kb in kbs]
        no_far = jnp.where(qi == 0, NO_KEYS, 0.0).astype(F32)
        rmin = None
        for h in range(2):
            c0, cs = sums[h]
            a0 = weights(z0_ref[slot, h], c0, mlo_ref[:, :TK])
            r = jnp.concatenate([jnp.zeros((1, TK), F32), c0[0:1, :]], axis=1)
            a1 = weights(z_ref[slot, h, 0], cs[0] + r, mlo_ref[...])
            r = r + cs[0][0:1, :] + no_far
            parts = [jnp.concatenate([jnp.zeros((TK, TK), BF16), a0], axis=1), a1]
            for j in range(N_FAR):
                c = cs[1 + j]
                parts.append(jnp.exp(z_ref[slot, h, 1 + j] - (c + r)).astype(BF16))
                r = r + c[0:1, :]
            hs = slice(h * HEAD_DIM, (h + 1) * HEAD_DIM)
            v_cat = jnp.concatenate([v[hs, :] for v in vs], axis=1)
            acc_ref[h] = jnp.dot(v_cat, jnp.concatenate(parts, axis=0), preferred_element_type=F32)
            r_ref[h] = r
            m = jnp.min(r)
            rmin = m if rmin is None else jnp.minimum(rmin, m)
        return rmin

    def far_block(kb, qh):
        kblk = k_block(kb)
        vblk = vt_ref[0, 0, kb]
        rmin = None
        for h in range(2):
            z = jnp.dot(kblk, qh[h], preferred_element_type=F32)
            c = cumsum(softplus(z).astype(BF16))
            r_old = r_ref[h]
            a = jnp.exp(z - (c + r_old)).astype(BF16)
            acc_ref[h] += jnp.dot(vblk[h * HEAD_DIM:(h + 1) * HEAD_DIM, :], a,
                                  preferred_element_type=F32)
            r_new = r_old + c[0:1, :]
            r_ref[h] = r_new
            m = jnp.min(r_new)
            rmin = m if rmin is None else jnp.minimum(rmin, m)
        return rmin

    def finish(qi, rmin0):
        qh = head_queries(qi)

        def cond(c):
            kb, rmin = c
            return jnp.logical_and(kb >= 0, rmin < SKIP_THRESHOLD)

        def body(c):
            kb, _ = c
            return kb - 1, far_block(kb, qh)

        lax.while_loop(cond, body, (2 * qi - 1 - N_FAR, rmin0))
        acc = jnp.concatenate([acc_ref[0], acc_ref[1]], axis=0)
        write_tile(qi, acc.T)

    return init, stage_a, stage_b1, stage_b2, finish


def _attn_mlp_kernel(qt_ref, k_ref, vt_ref, x_ref, ssm_ref, ga_ref, wout_ref, g2_ref, w1_ref, w2_ref,
                     o_ref, sb_ref, hn_ref, mlo_ref, z0_ref, sp0_ref, z_ref, sp_ref, acc_ref, r_ref):
    s, t = pl.program_id(0), pl.program_id(1)
    last = pl.num_programs(0) - 1
    nq = qt_ref.shape[2]
    wslot = lax.rem(s, 2)
    rslot = 1 - wslot

    def write_tile(qi, tile):
        sb_ref[wslot, t, pl.ds(pl.multiple_of(qi * TQ, TQ), TQ), :] = tile

    init, stage_a, stage_b1, stage_b2, finish = _attention_stages(
        qt_ref, k_ref, vt_ref, mlo_ref, z0_ref, sp0_ref, z_ref, sp_ref, acc_ref, r_ref, write_tile)

    def mlp_head():
        rows = pl.ds(pl.multiple_of(t * TM_MLP, TM_MLP), TM_MLP)
        sb = jnp.concatenate([sb_ref[rslot, p, rows, :] for p in range(HEAD_PAIRS)], axis=1)
        an = (_rms(sb) * ga_ref[...]).astype(BF16)
        h = x_ref[0] + jnp.dot(an, wout_ref[:SB_WIDTH, :], preferred_element_type=F32)
        h = h + jnp.dot(ssm_ref[0], wout_ref[SB_WIDTH:, :], preferred_element_type=F32)
        hn_ref[...] = (_rms(h) * g2_ref[...]).astype(BF16)
        o_ref[0] = h

    def mlp_chunk(c):
        a = jnp.dot(hn_ref[...], w1_ref[c], preferred_element_type=F32)
        a = jnp.square(jnp.maximum(a, 0.0)).astype(BF16)
        o_ref[0] += jnp.dot(a, w2_ref[c], preferred_element_type=F32)

    def step(attn, mlp):
        if attn:
            init()
            stage_a(0, 0)
        if mlp:
            mlp_head()

        def tile(qi, slot, next_slot, has_next=True):
            if attn:
                sums = stage_b1(slot)
                if has_next:
                    stage_a(qi + 1, next_slot)
            if mlp:
                mlp_chunk(qi)
            if attn:
                finish(qi, stage_b2(qi, slot, sums))

        def tile_pair(m, carry):
            tile(2 * m, 0, 1)
            tile(2 * m + 1, 1, 0)
            return carry

        lax.fori_loop(0, (nq - 2) // 2, tile_pair, 0)
        tile(nq - 2, 0, 1)
        tile(nq - 1, 1, 0, has_next=False)

    @pl.when(s == 0)
    def _():
        step(True, False)

    @pl.when(jnp.logical_and(s > 0, s < last))
    def _():
        step(True, True)

    @pl.when(s == last)
    def _():
        step(False, True)


def _attn_mlp(qt, k, vt, x, ssm, ga, wout, g2, w1, w2):
    B, _, nq, _, _ = qt.shape
    L, D = x.shape[1], x.shape[2]
    nk = vt.shape[2]
    assert nq % 2 == 0 and nq >= 2 and nq == D_FF // FF_SUB and L // TM_MLP == HEAD_PAIRS
    const = lambda s, t: (0, 0)
    const3 = lambda s, t: (0, 0, 0)
    cur = lambda s: jnp.minimum(s, B - 1)
    prv = lambda s: jnp.maximum(s - 1, 0)
    tok = lambda s, t: jnp.where(s == 0, 0, t)
    return pl.pallas_call(
        _attn_mlp_kernel,
        grid=(B + 1, HEAD_PAIRS),
        in_specs=[
            pl.BlockSpec((1, 1, nq, LANES, TQ), lambda s, t: (cur(s), t, 0, 0, 0)),
            pl.BlockSpec((1, L, LANES), lambda s, t: (cur(s), 0, t)),
            pl.BlockSpec((1, 1, nk, LANES, TK), lambda s, t: (cur(s), t, 0, 0, 0)),
            pl.BlockSpec((1, TM_MLP, D), lambda s, t: (prv(s), tok(s, t), 0)),
            pl.BlockSpec((1, TM_MLP, SSM_WIDTH), lambda s, t: (prv(s), tok(s, t), 0)),
            pl.BlockSpec((1, SB_WIDTH), const),
            pl.BlockSpec((D, D), const),
            pl.BlockSpec((1, D), const),
            pl.BlockSpec((D_FF // FF_SUB, D, FF_SUB), const3),
            pl.BlockSpec((D_FF // FF_SUB, FF_SUB, D), const3),
        ],
        out_specs=pl.BlockSpec((1, TM_MLP, D), lambda s, t: (prv(s), tok(s, t), 0)),
        out_shape=jax.ShapeDtypeStruct((B, L, D), F32),
        scratch_shapes=[
            pltpu.VMEM((2, HEAD_PAIRS, L, LANES), F32),
            pltpu.VMEM((TM_MLP, D), BF16),
            pltpu.VMEM((TK, TQ), F32),
            pltpu.VMEM((2, 2, TK, TK), F32),
            pltpu.VMEM((2, 2, TK, TK), BF16),
            pltpu.VMEM((2, 2, 1 + N_FAR, TK, TQ), F32),
            pltpu.VMEM((2, 2, 1 + N_FAR, TK, TQ), BF16),
            pltpu.VMEM((2, HEAD_DIM, TQ), F32),
            pltpu.VMEM((2, 1, TQ), F32),
        ],
        compiler_params=pltpu.CompilerParams(
            dimension_semantics=("arbitrary", "arbitrary"),
            vmem_limit_bytes=VMEM_LIMIT_BYTES),
        name="attn_mlp",
    )(qt, k, vt, x, ssm, ga, wout, g2, w1, w2)


def _s5_params(lam_re, lam_im, log_dt, b_re, b_im, c_re, c_im):
    G, P, H = SSM_GROUPS, SSM_STATE, SSM_GROUP
    lr, li = lam_re.astype(F32), lam_im.astype(F32)
    dt = jnp.exp(log_dt.astype(F32))[:, None]
    mag = jnp.exp(lr * dt)
    are, aim = mag * jnp.cos(li * dt), mag * jnp.sin(li * dt)
    den = lr * lr + li * li
    wr = ((are - 1.0) * lr + aim * li) / den
    wi = (aim * lr - (are - 1.0) * li) / den
    bbr = wr[:, :, None] * b_re.astype(F32) - wi[:, :, None] * b_im.astype(F32)
    bbi = wr[:, :, None] * b_im.astype(F32) + wi[:, :, None] * b_re.astype(F32)
    are, aim = are.reshape(1, G * P), aim.reshape(1, G * P)
    gpb = MXU_DIM // H
    eye = jnp.eye(gpb, dtype=F32)

    def b_layout(b):
        b = b.reshape(G // gpb, gpb, P, H)
        return jnp.einsum("kgph,gf->kghfp", b, eye).reshape(G // gpb, gpb * H, gpb * P).astype(BF16)

    def c_layout(c):
        c = c.reshape(G // gpb, gpb, H, P)
        return jnp.einsum("kghp,gf->kgpfh", c, eye).reshape(G // gpb, gpb * P, gpb * H).astype(BF16)

    return (b_layout(bbr), b_layout(bbi), are, aim,
            c_layout(c_re.astype(F32)), c_layout(-c_im.astype(F32)))


def kernel(x, norm1_g, w_in, q_norm_g, k_norm_g, ssm_lambda_re, ssm_lambda_im, ssm_log_dt,
           ssm_b_re, ssm_b_im, ssm_c_re, ssm_c_im, ssm_d, w_glu, b_glu, attn_out_g,
           ssm_out_g, w_out, norm2_g, w_mlp_in, w_mlp_out):
    B, L, D = x.shape
    assert (D, L % TM_PROJ, L % TM_MLP, L % TT) == (D_MODEL, 0, 0, 0)
    heads = SB_WIDTH // HEAD_DIM
    row = lambda g: g.astype(F32).reshape(1, -1)

    head_id = jnp.arange(SB_WIDTH) // HEAD_DIM
    hsel = (head_id[:, None] == head_id[None, :]).astype(BF16) * (1.0 / HEAD_DIM)
    gq = row(jnp.tile(q_norm_g.astype(F32), heads)) * (1.0 / math.sqrt(HEAD_DIM))
    gk = row(jnp.tile(k_norm_g.astype(F32), heads))

    qt, k, vt, u = _inproj(x.astype(F32), row(norm1_g), w_in.astype(BF16), hsel, gq, gk)

    bre, bim, are, aim, cre, cim = _s5_params(
        ssm_lambda_re, ssm_lambda_im, ssm_log_dt, ssm_b_re, ssm_b_im, ssm_c_re, ssm_c_im)
    ssm = _s5(u, bre, bim, are, aim, cre, cim,
              row(ssm_d), w_glu.astype(BF16), row(b_glu), row(ssm_out_g))

    chunks = D_FF // FF_SUB
    w1 = w_mlp_in.astype(BF16).reshape(D, chunks, FF_SUB).transpose(1, 0, 2)
    w2 = w_mlp_out.astype(BF16).reshape(chunks, FF_SUB, D)
    out = _attn_mlp(qt, k, vt, x.astype(F32), ssm, row(attn_out_g), w_out.astype(BF16),
                    row(norm2_g), w1, w2)
    return out.astype(x.dtype)
```

```python
import functools
import math

import jax
import jax.numpy as jnp
from jax import lax
from jax.experimental import pallas as pl
from jax.experimental.pallas import tpu as pltpu

F32 = jnp.float32
BF16 = jnp.bfloat16

D_MODEL = 1024
SB_WIDTH = 512
HEAD_DIM = 64
HEAD_PAIRS = SB_WIDTH // (2 * HEAD_DIM)
SSM_WIDTH = 512
SSM_GROUP = 16
SSM_GROUPS = 32
SSM_STATE = 64
SSM_REAL = SSM_GROUPS * SSM_STATE
D_FF = 4 * D_MODEL
EPS = 1e-6

LANES = 128
MXU_DIM = 256
VMEM_LIMIT_BYTES = 56 * 1024 * 1024

TM_PROJ = 512
TQ = 256
TK = 128
TT = 32
SCAN_W = 512
TM_MLP = 512
FF_SUB = 512

SKIP_THRESHOLD = 88.0
N_FAR = 1
NO_KEYS = 1e30


def _rms(x):
    return x * lax.rsqrt(jnp.mean(x * x, axis=-1, keepdims=True) + EPS)


def _inproj_kernel(x_ref, g1_ref, win_ref, hsel_ref, gq_ref, gk_ref,
                   qt_ref, k_ref, vt_ref, u_ref):
    x = x_ref[0]
    xn = (_rms(x) * g1_ref[...]).astype(BF16)
    proj = jnp.dot(xn, win_ref[...], preferred_element_type=F32)
    q = proj[:, 0 * SB_WIDTH:1 * SB_WIDTH]
    k = proj[:, 1 * SB_WIDTH:2 * SB_WIDTH]
    v = proj[:, 2 * SB_WIDTH:3 * SB_WIDTH]
    u = proj[:, 3 * SB_WIDTH:]
    hsel = hsel_ref[...]
    msq = jnp.dot((q * q).astype(BF16), hsel, preferred_element_type=F32)
    msk = jnp.dot((k * k).astype(BF16), hsel, preferred_element_type=F32)
    qn = q * lax.rsqrt(msq + EPS) * gq_ref[...]
    kn = k * lax.rsqrt(msk + EPS) * gk_ref[...]
    k_ref[0] = kn.astype(BF16)
    u_ref[0] = u.astype(BF16)
    for j in range(TM_PROJ // TQ):
        qt = qn[j * TQ:(j + 1) * TQ, :].T
        qt_ref[0, :, j] = qt.astype(BF16).reshape(HEAD_PAIRS, LANES, TQ)
    for j in range(TM_PROJ // TK):
        vt = v[j * TK:(j + 1) * TK, :].T
        vt_ref[0, :, j] = vt.astype(BF16).reshape(HEAD_PAIRS, LANES, TK)


def _inproj(x, g1, win, hsel, gq, gk):
    B, L, D = x.shape
    nq, nk = L // TQ, L // TK
    const = lambda b, t: (0, 0)
    return pl.pallas_call(
        _inproj_kernel,
        grid=(B, L // TM_PROJ),
        in_specs=[
            pl.BlockSpec((1, TM_PROJ, D), lambda b, t: (b, t, 0)),
            pl.BlockSpec((1, D), const),
            pl.BlockSpec((D, 4 * SB_WIDTH), const),
            pl.BlockSpec((SB_WIDTH, SB_WIDTH), const),
            pl.BlockSpec((1, SB_WIDTH), const),
            pl.BlockSpec((1, SB_WIDTH), const),
        ],
        out_specs=[
            pl.BlockSpec((1, HEAD_PAIRS, TM_PROJ // TQ, LANES, TQ), lambda b, t: (b, 0, t, 0, 0)),
            pl.BlockSpec((1, TM_PROJ, SB_WIDTH), lambda b, t: (b, t, 0)),
            pl.BlockSpec((1, HEAD_PAIRS, TM_PROJ // TK, LANES, TK), lambda b, t: (b, 0, t, 0, 0)),
            pl.BlockSpec((1, TM_PROJ, SSM_WIDTH), lambda b, t: (b, t, 0)),
        ],
        out_shape=[
            jax.ShapeDtypeStruct((B, HEAD_PAIRS, nq, LANES, TQ), BF16),
            jax.ShapeDtypeStruct((B, L, SB_WIDTH), BF16),
            jax.ShapeDtypeStruct((B, HEAD_PAIRS, nk, LANES, TK), BF16),
            jax.ShapeDtypeStruct((B, L, SSM_WIDTH), BF16),
        ],
        compiler_params=pltpu.CompilerParams(
            dimension_semantics=("parallel", "parallel"),
            vmem_limit_bytes=VMEM_LIMIT_BYTES),
        name="inproj",
    )(x, g1, win, hsel, gq, gk)


def _gelu_tanh(x):
    c = math.sqrt(2.0 / math.pi)
    return 0.5 * x * (1.0 + jnp.tanh(c * (x + 0.044715 * (x * x * x))))


def _s5_kernel(u_ref, perm_ref, permt_ref, bre_ref, bim_ref, are_ref, aim_ref, cre_ref, cim_ref,
               d_ref, wglu_ref, bglu_ref, gs_ref, o_ref, ub_ref, bu_ref, xs_ref, st_ref, *, batch):
    s = pl.program_id(0)

    @pl.when(s == 0)
    def _():
        for ref in (ub_ref, bu_ref, xs_ref, st_ref):
            ref[...] = jnp.zeros_like(ref)

    rows = batch * TT
    half = SSM_REAL // 2
    cur = lax.rem(s, 2)
    prev = 1 - cur

    u = jnp.dot(perm_ref[...], u_ref[...].reshape(rows, SSM_WIDTH),
                preferred_element_type=F32).astype(BF16)
    ub_ref[lax.rem(s, 3)] = u
    for kb in range(2):
        ub = u[:, kb * MXU_DIM:(kb + 1) * MXU_DIM]
        bu_ref[cur, :, kb * half:(kb + 1) * half] = jnp.dot(
            ub, bre_ref[kb], preferred_element_type=F32)
        bu_ref[cur, :, SSM_REAL + kb * half:SSM_REAL + (kb + 1) * half] = jnp.dot(
            ub, bim_ref[kb], preferred_element_type=F32)

    ys = []
    for ob in range(2):
        y = jnp.dot(xs_ref[cur, :, ob * half:(ob + 1) * half], cre_ref[ob],
                    preferred_element_type=F32)
        y = y + jnp.dot(xs_ref[cur, :, SSM_REAL + ob * half:SSM_REAL + (ob + 1) * half],
                        cim_ref[ob], preferred_element_type=F32)
        ys.append(y)
    y = jnp.concatenate(ys, axis=1) + d_ref[...] * ub_ref[lax.rem(s + 1, 3)].astype(F32)
    y = _gelu_tanh(y)
    gate = jax.nn.sigmoid(
        jnp.dot(y.astype(BF16), wglu_ref[...], preferred_element_type=F32) + bglu_ref[...])
    out_tb = (_rms(y * gate) * gs_ref[...]).astype(BF16)
    out_bt = jnp.dot(permt_ref[...], out_tb, preferred_element_type=F32)
    o_ref[...] = out_bt.astype(o_ref.dtype).reshape(batch, TT, SSM_WIDTH)

    for sc in range(SSM_REAL // SCAN_W):
        re = slice(sc * SCAN_W, (sc + 1) * SCAN_W)
        im = slice(SSM_REAL + sc * SCAN_W, SSM_REAL + (sc + 1) * SCAN_W)
        ar = jnp.broadcast_to(are_ref[:, re], (batch, SCAN_W))
        ai = jnp.broadcast_to(aim_ref[:, re], (batch, SCAN_W))

        def step(t, carry, re=re, im=im, ar=ar, ai=ai):
            xr, xi = carry
            rs = pl.ds(pl.multiple_of(t * batch, batch), batch)
            nxr = ar * xr - ai * xi + bu_ref[prev, rs, re]
            nxi = ar * xi + ai * xr + bu_ref[prev, rs, im]
            xs_ref[prev, rs, re] = nxr.astype(BF16)
            xs_ref[prev, rs, im] = nxi.astype(BF16)
            return nxr, nxi

        xr, xi = lax.fori_loop(0, TT, step, (st_ref[:, re], st_ref[:, im]), unroll=4)
        st_ref[:, re] = xr
        st_ref[:, im] = xi


def _s5(u, bre, bim, are, aim, cre, cim, d, wglu, bglu, gs):
    batch, L, _ = u.shape
    rows = TT * batch
    tiles = L // TT
    half = SSM_REAL // 2
    c2 = lambda s: (0, 0)
    c3 = lambda s: (0, 0, 0)
    src = (jnp.arange(rows) % batch) * TT + jnp.arange(rows) // batch
    perm = (src[:, None] == jnp.arange(rows)[None, :]).astype(BF16)
    return pl.pallas_call(
        functools.partial(_s5_kernel, batch=batch),
        grid=(tiles + 2,),
        in_specs=[
            pl.BlockSpec((batch, TT, SSM_WIDTH), lambda s: (0, jnp.minimum(s, tiles - 1), 0)),
            pl.BlockSpec((rows, rows), c2),
            pl.BlockSpec((rows, rows), c2),
            pl.BlockSpec((2, MXU_DIM, half), c3),
            pl.BlockSpec((2, MXU_DIM, half), c3),
            pl.BlockSpec((1, SSM_REAL), c2),
            pl.BlockSpec((1, SSM_REAL), c2),
            pl.BlockSpec((2, half, MXU_DIM), c3),
            pl.BlockSpec((2, half, MXU_DIM), c3),
            pl.BlockSpec((1, SSM_WIDTH), c2),
            pl.BlockSpec((SSM_WIDTH, SSM_WIDTH), c2),
            pl.BlockSpec((1, SSM_WIDTH), c2),
            pl.BlockSpec((1, SSM_WIDTH), c2),
        ],
        out_specs=pl.BlockSpec((batch, TT, SSM_WIDTH), lambda s: (0, jnp.maximum(s - 2, 0), 0)),
        out_shape=jax.ShapeDtypeStruct((batch, L, SSM_WIDTH), BF16),
        scratch_shapes=[
            pltpu.VMEM((3, rows, SSM_WIDTH), BF16),
            pltpu.VMEM((2, rows, 2 * SSM_REAL), F32),
            pltpu.VMEM((2, rows, 2 * SSM_REAL), BF16),
            pltpu.VMEM((batch, 2 * SSM_REAL), F32),
        ],
        compiler_params=pltpu.CompilerParams(
            dimension_semantics=("arbitrary",),
            vmem_limit_bytes=VMEM_LIMIT_BYTES),
        name="s5_glu",
    )(u, perm, perm.T, bre, bim, are, aim, cre, cim, d, wglu, bglu, gs)


def _attention_stages(qt_ref, k_ref, vt_ref, mlo_ref, z0_ref, sp0_ref, z_ref, sp_ref, acc_ref,
                      r_ref, rall_ref, rmin_ref, write_tile, add_tile):
    srow = lax.broadcasted_iota(jnp.int32, (TK, TK), 0)
    scol = lax.broadcasted_iota(jnp.int32, (TK, TK), 1)
    tri = jnp.where(scol >= srow, 1.0, 0.0).astype(BF16)
    drow = lax.broadcasted_iota(jnp.int32, (LANES, TQ), 0)

    def init():
        row = lax.broadcasted_iota(jnp.int32, (TK, TQ), 0)
        col = lax.broadcasted_iota(jnp.int32, (TK, TQ), 1)
        mlo_ref[...] = jnp.where(row < col, 1.0, 0.0)

    def weights(z, s, mask):
        return (jnp.exp(jnp.minimum(z - s, 0.0)) * mask).astype(BF16)

    def k_block(kb):
        return k_ref[0, pl.ds(pl.multiple_of(kb * TK, TK), TK), :]

    def softplus(z):
        return jnp.maximum(z, 0.0) + jnp.log(1.0 + jnp.exp(-jnp.abs(z)))

    def cumsum(sp):
        return jnp.dot(tri, sp, preferred_element_type=F32)

    def head_queries(qi):
        qt = qt_ref[0, 0, qi]
        zero = jnp.zeros_like(qt)
        return (jnp.where(drow < HEAD_DIM, qt, zero), jnp.where(drow >= HEAD_DIM, qt, zero))

    def far_kb(qi, j):
        return jnp.maximum(2 * qi - 1 - j, 0)

    def stage_a(qi, slot):
        qh = head_queries(qi)
        k_top, k_lo = k_block(2 * qi + 1), k_block(2 * qi)
        k_far = [k_block(far_kb(qi, j)) for j in range(N_FAR)]
        for h in range(2):
            z0 = jnp.dot(k_top, qh[h][:, TK:], preferred_element_type=F32)
            z0_ref[slot, h] = z0
            sp0_ref[slot, h] = (softplus(z0) * mlo_ref[:, :TK]).astype(BF16)
            z1 = jnp.dot(k_lo, qh[h], preferred_element_type=F32)
            z_ref[slot, h, 0] = z1
            sp_ref[slot, h, 0] = (softplus(z1) * mlo_ref[...]).astype(BF16)
            for j in range(N_FAR):
                z = jnp.dot(k_far[j], qh[h], preferred_element_type=F32)
                z_ref[slot, h, 1 + j] = z
                sp_ref[slot, h, 1 + j] = softplus(z).astype(BF16)

    def stage_b1(slot):
        return [(cumsum(sp0_ref[slot, h]), [cumsum(sp_ref[slot, h, j]) for j in range(1 + N_FAR)])
                for h in range(2)]

    def stage_b2(qi, slot, sums):
        kbs = [2 * qi + 1, 2 * qi] + [far_kb(qi, j) for j in range(N_FAR)]
        vs = [vt_ref[0, 0, kb] for kb in kbs]
        no_far = jnp.where(qi == 0, NO_KEYS, 0.0).astype(F32)
        rmin, accs = None, []
        for h in range(2):
            c0, cs = sums[h]
            a0 = weights(z0_ref[slot, h], c0, mlo_ref[:, :TK])
            r = jnp.concatenate([jnp.zeros((1, TK), F32), c0[0:1, :]], axis=1)
            a1 = weights(z_ref[slot, h, 0], cs[0] + r, mlo_ref[...])
            r = r + cs[0][0:1, :] + no_far
            parts = [jnp.concatenate([jnp.zeros((TK, TK), BF16), a0], axis=1), a1]
            for j in range(N_FAR):
                c = cs[1 + j]
                parts.append(jnp.exp(z_ref[slot, h, 1 + j] - (c + r)).astype(BF16))
                r = r + c[0:1, :]
            hs = slice(h * HEAD_DIM, (h + 1) * HEAD_DIM)
            v_cat = jnp.concatenate([v[hs, :] for v in vs], axis=1)
            accs.append(jnp.dot(v_cat, jnp.concatenate(parts, axis=0),
                                preferred_element_type=F32))
            rall_ref[qi, h] = r
            m = jnp.min(r)
            rmin = m if rmin is None else jnp.minimum(rmin, m)
        rmin_ref[qi] = rmin
        write_tile(qi, jnp.concatenate(accs, axis=0).T)

    def far_block(kb, qh):
        kblk = k_block(kb)
        vblk = vt_ref[0, 0, kb]
        rmin = None
        zs = [jnp.dot(kblk, qh[h], preferred_element_type=F32) for h in range(2)]
        cs = [cumsum(softplus(z).astype(BF16)) for z in zs]
        for h in range(2):
            z, c = zs[h], cs[h]
            r_old = r_ref[h]
            a = jnp.exp(z - (c + r_old)).astype(BF16)
            acc_ref[h] += jnp.dot(vblk[h * HEAD_DIM:(h + 1) * HEAD_DIM, :], a,
                                  preferred_element_type=F32)
            r_new = r_old + c[0:1, :]
            r_ref[h] = r_new
            m = jnp.min(r_new)
            rmin = m if rmin is None else jnp.minimum(rmin, m)
        return rmin

    def finish(nq):
        def one_tile(qi, carry):
            kb0 = 2 * qi - 1 - N_FAR
            rmin0 = rmin_ref[qi]

            @pl.when(jnp.logical_and(kb0 >= 0, rmin0 < SKIP_THRESHOLD))
            def _():
                qh = head_queries(qi)
                acc_ref[...] = jnp.zeros_like(acc_ref)
                for h in range(2):
                    r_ref[h] = rall_ref[qi, h]

                def cond(c):
                    kb, rmin = c
                    return jnp.logical_and(kb >= 0, rmin < SKIP_THRESHOLD)

                def body(c):
                    kb, _ = c
                    return kb - 1, far_block(kb, qh)

                lax.while_loop(cond, body, (kb0, rmin0))
                acc = jnp.concatenate([acc_ref[0], acc_ref[1]], axis=0)
                add_tile(qi, acc.T)

            return carry

        lax.fori_loop(0, nq, one_tile, 0)

    return init, stage_a, stage_b1, stage_b2, finish


def _attn_mlp_kernel(qt_ref, k_ref, vt_ref, x_ref, ssm_ref, ga_ref, wout_ref, g2_ref, w1_ref, w2_ref,
                     o_ref, sb_ref, hn_ref, mlo_ref, z0_ref, sp0_ref, z_ref, sp_ref, acc_ref, r_ref,
                     rall_ref, rmin_ref):
    s, t = pl.program_id(0), pl.program_id(1)
    last = pl.num_programs(0) - 1
    nq = qt_ref.shape[2]
    wslot = lax.rem(s, 2)
    rslot = 1 - wslot

    def tile_rows(qi):
        return pl.ds(pl.multiple_of(qi * TQ, TQ), TQ)

    def write_tile(qi, tile):
        sb_ref[wslot, t, tile_rows(qi), :] = tile

    def add_tile(qi, tile):
        sb_ref[wslot, t, tile_rows(qi), :] += tile

    init, stage_a, stage_b1, stage_b2, finish = _attention_stages(
        qt_ref, k_ref, vt_ref, mlo_ref, z0_ref, sp0_ref, z_ref, sp_ref, acc_ref, r_ref, rall_ref,
        rmin_ref, write_tile, add_tile)

    def mlp_head():
        rows = pl.ds(pl.multiple_of(t * TM_MLP, TM_MLP), TM_MLP)
        sb = jnp.concatenate([sb_ref[rslot, p, rows, :] for p in range(HEAD_PAIRS)], axis=1)
        an = (_rms(sb) * ga_ref[...]).astype(BF16)
        h = x_ref[0] + jnp.dot(an, wout_ref[:SB_WIDTH, :], preferred_element_type=F32)
        h = h + jnp.dot(ssm_ref[0], wout_ref[SB_WIDTH:, :], preferred_element_type=F32)
        hn_ref[...] = (_rms(h) * g2_ref[...]).astype(BF16)
        o_ref[0] = h

    def mlp_chunk(c):
        a = jnp.dot(hn_ref[...], w1_ref[c], preferred_element_type=F32)
        a = jnp.square(jnp.maximum(a, 0.0)).astype(BF16)
        o_ref[0] += jnp.dot(a, w2_ref[c], preferred_element_type=F32)

    def step(attn, mlp):
        if attn:
            init()
            stage_a(0, 0)
        if mlp:
            mlp_head()

        def tile(qi, slot, next_slot, has_next=True):
            if attn:
                sums = stage_b1(slot)
                if has_next:
                    stage_a(qi + 1, next_slot)
            if mlp:
                mlp_chunk(qi)
            if attn:
                stage_b2(qi, slot, sums)

        def tile_pair(m, carry):
            tile(2 * m, 0, 1)
            tile(2 * m + 1, 1, 0)
            return carry

        lax.fori_loop(0, (nq - 2) // 2, tile_pair, 0)
        tile(nq - 2, 0, 1)
        tile(nq - 1, 1, 0, has_next=False)
        if attn:
            finish(nq)

    @pl.when(s == 0)
    def _():
        step(True, False)

    @pl.when(jnp.logical_and(s > 0, s < last))
    def _():
        step(True, True)

    @pl.when(s == last)
    def _():
        step(False, True)


def _attn_mlp(qt, k, vt, x, ssm, ga, wout, g2, w1, w2):
    B, _, nq, _, _ = qt.shape
    L, D = x.shape[1], x.shape[2]
    nk = vt.shape[2]
    assert nq % 2 == 0 and nq >= 2 and nq == D_FF // FF_SUB and L // TM_MLP == HEAD_PAIRS
    const = lambda s, t: (0, 0)
    const3 = lambda s, t: (0, 0, 0)
    cur = lambda s: jnp.minimum(s, B - 1)
    prv = lambda s: jnp.maximum(s - 1, 0)
    tok = lambda s, t: jnp.where(s == 0, 0, t)
    return pl.pallas_call(
        _attn_mlp_kernel,
        grid=(B + 1, HEAD_PAIRS),
        in_specs=[
            pl.BlockSpec((1, 1, nq, LANES, TQ), lambda s, t: (cur(s), t, 0, 0, 0)),
            pl.BlockSpec((1, L, LANES), lambda s, t: (cur(s), 0, t)),
            pl.BlockSpec((1, 1, nk, LANES, TK), lambda s, t: (cur(s), t, 0, 0, 0)),
            pl.BlockSpec((1, TM_MLP, D), lambda s, t: (prv(s), tok(s, t), 0)),
            pl.BlockSpec((1, TM_MLP, SSM_WIDTH), lambda s, t: (prv(s), tok(s, t), 0)),
            pl.BlockSpec((1, SB_WIDTH), const),
            pl.BlockSpec((D, D), const),
            pl.BlockSpec((1, D), const),
            pl.BlockSpec((D_FF // FF_SUB, D, FF_SUB), const3),
            pl.BlockSpec((D_FF // FF_SUB, FF_SUB, D), const3),
        ],
        out_specs=pl.BlockSpec((1, TM_MLP, D), lambda s, t: (prv(s), tok(s, t), 0)),
        out_shape=jax.ShapeDtypeStruct((B, L, D), F32),
        scratch_shapes=[
            pltpu.VMEM((2, HEAD_PAIRS, L, LANES), F32),
            pltpu.VMEM((TM_MLP, D), BF16),
            pltpu.VMEM((TK, TQ), F32),
            pltpu.VMEM((2, 2, TK, TK), F32),
            pltpu.VMEM((2, 2, TK, TK), BF16),
            pltpu.VMEM((2, 2, 1 + N_FAR, TK, TQ), F32),
            pltpu.VMEM((2, 2, 1 + N_FAR, TK, TQ), BF16),
            pltpu.VMEM((2, HEAD_DIM, TQ), F32),
            pltpu.VMEM((2, 1, TQ), F32),
            pltpu.VMEM((nq, 2, 1, TQ), F32),
            pltpu.SMEM((nq,), F32),
        ],
        compiler_params=pltpu.CompilerParams(
            dimension_semantics=("arbitrary", "arbitrary"),
            vmem_limit_bytes=VMEM_LIMIT_BYTES),
        name="attn_mlp",
    )(qt, k, vt, x, ssm, ga, wout, g2, w1, w2)


def _s5_params(lam_re, lam_im, log_dt, b_re, b_im, c_re, c_im):
    G, P, H = SSM_GROUPS, SSM_STATE, SSM_GROUP
    lr, li = lam_re.astype(F32), lam_im.astype(F32)
    dt = jnp.exp(log_dt.astype(F32))[:, None]
    mag = jnp.exp(lr * dt)
    are, aim = mag * jnp.cos(li * dt), mag * jnp.sin(li * dt)
    den = lr * lr + li * li
    wr = ((are - 1.0) * lr + aim * li) / den
    wi = (aim * lr - (are - 1.0) * li) / den
    bbr = wr[:, :, None] * b_re.astype(F32) - wi[:, :, None] * b_im.astype(F32)
    bbi = wr[:, :, None] * b_im.astype(F32) + wi[:, :, None] * b_re.astype(F32)
    are, aim = are.reshape(1, G * P), aim.reshape(1, G * P)
    gpb = MXU_DIM // H
    eye = jnp.eye(gpb, dtype=F32)

    def b_layout(b):
        b = b.reshape(G // gpb, gpb, P, H)
        return jnp.einsum("kgph,gf->kghfp", b, eye).reshape(G // gpb, gpb * H, gpb * P).astype(BF16)

    def c_layout(c):
        c = c.reshape(G // gpb, gpb, H, P)
        return jnp.einsum("kghp,gf->kgpfh", c, eye).reshape(G // gpb, gpb * P, gpb * H).astype(BF16)

    return (b_layout(bbr), b_layout(bbi), are, aim,
            c_layout(c_re.astype(F32)), c_layout(-c_im.astype(F32)))


def kernel(x, norm1_g, w_in, q_norm_g, k_norm_g, ssm_lambda_re, ssm_lambda_im, ssm_log_dt,
           ssm_b_re, ssm_b_im, ssm_c_re, ssm_c_im, ssm_d, w_glu, b_glu, attn_out_g,
           ssm_out_g, w_out, norm2_g, w_mlp_in, w_mlp_out):
    B, L, D = x.shape
    assert (D, L % TM_PROJ, L % TM_MLP, L % TT) == (D_MODEL, 0, 0, 0)
    heads = SB_WIDTH // HEAD_DIM
    row = lambda g: g.astype(F32).reshape(1, -1)

    head_id = jnp.arange(SB_WIDTH) // HEAD_DIM
    hsel = (head_id[:, None] == head_id[None, :]).astype(BF16) * (1.0 / HEAD_DIM)
    gq = row(jnp.tile(q_norm_g.astype(F32), heads)) * (1.0 / math.sqrt(HEAD_DIM))
    gk = row(jnp.tile(k_norm_g.astype(F32), heads))

    qt, k, vt, u = _inproj(x.astype(F32), row(norm1_g), w_in.astype(BF16), hsel, gq, gk)

    bre, bim, are, aim, cre, cim = _s5_params(
        ssm_lambda_re, ssm_lambda_im, ssm_log_dt, ssm_b_re, ssm_b_im, ssm_c_re, ssm_c_im)
    ssm = _s5(u, bre, bim, are, aim, cre, cim,
              row(ssm_d), w_glu.astype(BF16), row(b_glu), row(ssm_out_g))

    chunks = D_FF // FF_SUB
    w1 = w_mlp_in.astype(BF16).reshape(D, chunks, FF_SUB).transpose(1, 0, 2)
    w2 = w_mlp_out.astype(BF16).reshape(chunks, FF_SUB, D)
    out = _attn_mlp(qt, k, vt, x.astype(F32), ssm, row(attn_out_g), w_out.astype(BF16),
                    row(norm2_g), w1, w2)
    return out.astype(x.dtype)
```

```python
import functools
import math

import jax
import jax.numpy as jnp
from jax import lax
from jax.experimental import pallas as pl
from jax.experimental.pallas import tpu as pltpu

F32 = jnp.float32
BF16 = jnp.bfloat16

D_MODEL = 1024
SB_WIDTH = 512
HEAD_DIM = 64
HEAD_PAIRS = SB_WIDTH // (2 * HEAD_DIM)
SSM_WIDTH = 512
SSM_GROUP = 16
SSM_GROUPS = 32
SSM_STATE = 64
SSM_REAL = SSM_GROUPS * SSM_STATE
D_FF = 4 * D_MODEL
EPS = 1e-6

LANES = 128
MXU_DIM = 256
VMEM_LIMIT_BYTES = 56 * 1024 * 1024

TM_PROJ = 512
TQ = 256
TK = 128
TT = 32
SCAN_W = 512
TM_MLP = 512
FF_SUB = 512

SKIP_THRESHOLD = 88.0
N_FAR = 1
NO_KEYS = 1e30


def _rms(x):
    return x * lax.rsqrt(jnp.mean(x * x, axis=-1, keepdims=True) + EPS)


def _inproj_kernel(x_ref, g1_ref, win_ref, hsel_ref, gq_ref, gk_ref,
                   qt_ref, k_ref, vt_ref, u_ref):
    x = x_ref[0]
    xn = (_rms(x) * g1_ref[...]).astype(BF16)
    proj = jnp.dot(xn, win_ref[...], preferred_element_type=F32)
    q = proj[:, 0 * SB_WIDTH:1 * SB_WIDTH]
    k = proj[:, 1 * SB_WIDTH:2 * SB_WIDTH]
    v = proj[:, 2 * SB_WIDTH:3 * SB_WIDTH]
    u = proj[:, 3 * SB_WIDTH:]
    hsel = hsel_ref[...]
    msq = jnp.dot((q * q).astype(BF16), hsel, preferred_element_type=F32)
    msk = jnp.dot((k * k).astype(BF16), hsel, preferred_element_type=F32)
    qn = q * lax.rsqrt(msq + EPS) * gq_ref[...]
    kn = k * lax.rsqrt(msk + EPS) * gk_ref[...]
    k_ref[0] = kn.astype(BF16)
    u_ref[0] = u.astype(BF16)
    for j in range(TM_PROJ // TQ):
        qt = qn[j * TQ:(j + 1) * TQ, :].T
        qt_ref[0, :, j] = qt.astype(BF16).reshape(HEAD_PAIRS, LANES, TQ)
    for j in range(TM_PROJ // TK):
        vt = v[j * TK:(j + 1) * TK, :].T
        vt_ref[0, :, j] = vt.astype(BF16).reshape(HEAD_PAIRS, LANES, TK)


def _inproj(x, g1, win, hsel, gq, gk):
    B, L, D = x.shape
    nq, nk = L // TQ, L // TK
    const = lambda b, t: (0, 0)
    return pl.pallas_call(
        _inproj_kernel,
        grid=(B, L // TM_PROJ),
        in_specs=[
            pl.BlockSpec((1, TM_PROJ, D), lambda b, t: (b, t, 0)),
            pl.BlockSpec((1, D), const),
            pl.BlockSpec((D, 4 * SB_WIDTH), const),
            pl.BlockSpec((SB_WIDTH, SB_WIDTH), const),
            pl.BlockSpec((1, SB_WIDTH), const),
            pl.BlockSpec((1, SB_WIDTH), const),
        ],
        out_specs=[
            pl.BlockSpec((1, HEAD_PAIRS, TM_PROJ // TQ, LANES, TQ), lambda b, t: (b, 0, t, 0, 0)),
            pl.BlockSpec((1, TM_PROJ, SB_WIDTH), lambda b, t: (b, t, 0)),
            pl.BlockSpec((1, HEAD_PAIRS, TM_PROJ // TK, LANES, TK), lambda b, t: (b, 0, t, 0, 0)),
            pl.BlockSpec((1, TM_PROJ, SSM_WIDTH), lambda b, t: (b, t, 0)),
        ],
        out_shape=[
            jax.ShapeDtypeStruct((B, HEAD_PAIRS, nq, LANES, TQ), BF16),
            jax.ShapeDtypeStruct((B, L, SB_WIDTH), BF16),
            jax.ShapeDtypeStruct((B, HEAD_PAIRS, nk, LANES, TK), BF16),
            jax.ShapeDtypeStruct((B, L, SSM_WIDTH), BF16),
        ],
        compiler_params=pltpu.CompilerParams(
            dimension_semantics=("parallel", "parallel"),
            vmem_limit_bytes=VMEM_LIMIT_BYTES),
        name="inproj",
    )(x, g1, win, hsel, gq, gk)


def _gelu_tanh(x):
    c = math.sqrt(2.0 / math.pi)
    return 0.5 * x * (1.0 + jnp.tanh(c * (x + 0.044715 * (x * x * x))))


def _s5_kernel(u_ref, perm_ref, permt_ref, bre_ref, bim_ref, are_ref, aim_ref, cre_ref, cim_ref,
               d_ref, wglu_ref, bglu_ref, gs_ref, o_ref, ub_ref, bu_ref, xs_ref, st_ref, *, batch):
    s = pl.program_id(0)

    @pl.when(s == 0)
    def _():
        for ref in (ub_ref, bu_ref, xs_ref, st_ref):
            ref[...] = jnp.zeros_like(ref)

    rows = batch * TT
    half = SSM_REAL // 2
    cur = lax.rem(s, 2)
    prev = 1 - cur

    u = jnp.dot(perm_ref[...], u_ref[...].reshape(rows, SSM_WIDTH),
                preferred_element_type=F32).astype(BF16)
    ub_ref[lax.rem(s, 3)] = u
    for kb in range(2):
        ub = u[:, kb * MXU_DIM:(kb + 1) * MXU_DIM]
        bu_ref[cur, :, kb * half:(kb + 1) * half] = jnp.dot(
            ub, bre_ref[kb], preferred_element_type=F32)
        bu_ref[cur, :, SSM_REAL + kb * half:SSM_REAL + (kb + 1) * half] = jnp.dot(
            ub, bim_ref[kb], preferred_element_type=F32)

    ys = []
    for ob in range(2):
        y = jnp.dot(xs_ref[cur, :, ob * half:(ob + 1) * half], cre_ref[ob],
                    preferred_element_type=F32)
        y = y + jnp.dot(xs_ref[cur, :, SSM_REAL + ob * half:SSM_REAL + (ob + 1) * half],
                        cim_ref[ob], preferred_element_type=F32)
        ys.append(y)
    y = jnp.concatenate(ys, axis=1) + d_ref[...] * ub_ref[lax.rem(s + 1, 3)].astype(F32)
    y = _gelu_tanh(y)
    gate = jax.nn.sigmoid(
        jnp.dot(y.astype(BF16), wglu_ref[...], preferred_element_type=F32) + bglu_ref[...])
    out_tb = (_rms(y * gate) * gs_ref[...]).astype(BF16)
    out_bt = jnp.dot(permt_ref[...], out_tb, preferred_element_type=F32)
    o_ref[...] = out_bt.astype(o_ref.dtype).reshape(batch, TT, SSM_WIDTH)

    for sc in range(SSM_REAL // SCAN_W):
        re = slice(sc * SCAN_W, (sc + 1) * SCAN_W)
        im = slice(SSM_REAL + sc * SCAN_W, SSM_REAL + (sc + 1) * SCAN_W)
        ar = jnp.broadcast_to(are_ref[:, re], (batch, SCAN_W))
        ai = jnp.broadcast_to(aim_ref[:, re], (batch, SCAN_W))

        def step(t, carry, re=re, im=im, ar=ar, ai=ai):
            xr, xi = carry
            rs = pl.ds(pl.multiple_of(t * batch, batch), batch)
            nxr = ar * xr - ai * xi + bu_ref[prev, rs, re]
            nxi = ar * xi + ai * xr + bu_ref[prev, rs, im]
            xs_ref[prev, rs, re] = nxr.astype(BF16)
            xs_ref[prev, rs, im] = nxi.astype(BF16)
            return nxr, nxi

        xr, xi = lax.fori_loop(0, TT, step, (st_ref[:, re], st_ref[:, im]), unroll=4)
        st_ref[:, re] = xr
        st_ref[:, im] = xi


def _s5(u, bre, bim, are, aim, cre, cim, d, wglu, bglu, gs):
    batch, L, _ = u.shape
    rows = TT * batch
    tiles = L // TT
    half = SSM_REAL // 2
    c2 = lambda s: (0, 0)
    c3 = lambda s: (0, 0, 0)
    src = (jnp.arange(rows) % batch) * TT + jnp.arange(rows) // batch
    perm = (src[:, None] == jnp.arange(rows)[None, :]).astype(BF16)
    return pl.pallas_call(
        functools.partial(_s5_kernel, batch=batch),
        grid=(tiles + 2,),
        in_specs=[
            pl.BlockSpec((batch, TT, SSM_WIDTH), lambda s: (0, jnp.minimum(s, tiles - 1), 0)),
            pl.BlockSpec((rows, rows), c2),
            pl.BlockSpec((rows, rows), c2),
            pl.BlockSpec((2, MXU_DIM, half), c3),
            pl.BlockSpec((2, MXU_DIM, half), c3),
            pl.BlockSpec((1, SSM_REAL), c2),
            pl.BlockSpec((1, SSM_REAL), c2),
            pl.BlockSpec((2, half, MXU_DIM), c3),
            pl.BlockSpec((2, half, MXU_DIM), c3),
            pl.BlockSpec((1, SSM_WIDTH), c2),
            pl.BlockSpec((SSM_WIDTH, SSM_WIDTH), c2),
            pl.BlockSpec((1, SSM_WIDTH), c2),
            pl.BlockSpec((1, SSM_WIDTH), c2),
        ],
        out_specs=pl.BlockSpec((batch, TT, SSM_WIDTH), lambda s: (0, jnp.maximum(s - 2, 0), 0)),
        out_shape=jax.ShapeDtypeStruct((batch, L, SSM_WIDTH), BF16),
        scratch_shapes=[
            pltpu.VMEM((3, rows, SSM_WIDTH), BF16),
            pltpu.VMEM((2, rows, 2 * SSM_REAL), F32),
            pltpu.VMEM((2, rows, 2 * SSM_REAL), BF16),
            pltpu.VMEM((batch, 2 * SSM_REAL), F32),
        ],
        compiler_params=pltpu.CompilerParams(
            dimension_semantics=("arbitrary",),
            vmem_limit_bytes=VMEM_LIMIT_BYTES),
        name="s5_glu",
    )(u, perm, perm.T, bre, bim, are, aim, cre, cim, d, wglu, bglu, gs)


def _attention_stages(qt_ref, k_ref, vt_ref, mlo_ref, z0_ref, sp0_ref, z_ref, sp_ref, acc_ref,
                      r_ref, rall_ref, rmin_ref, write_tile, add_tile):
    srow = lax.broadcasted_iota(jnp.int32, (TK, TK), 0)
    scol = lax.broadcasted_iota(jnp.int32, (TK, TK), 1)
    tri = jnp.where(scol >= srow, 1.0, 0.0).astype(BF16)
    drow = lax.broadcasted_iota(jnp.int32, (LANES, TQ), 0)

    def init():
        row = lax.broadcasted_iota(jnp.int32, (TK, TQ), 0)
        col = lax.broadcasted_iota(jnp.int32, (TK, TQ), 1)
        mlo_ref[...] = jnp.where(row < col, 1.0, 0.0)

    def weights(z, s, mask):
        return (jnp.exp(jnp.minimum(z - s, 0.0)) * mask).astype(BF16)

    def k_block(kb):
        return k_ref[0, pl.ds(pl.multiple_of(kb * TK, TK), TK), :]

    def softplus(z):
        return jnp.maximum(z, 0.0) + jnp.log(1.0 + jnp.exp(-jnp.abs(z)))

    def cumsum(sp):
        return jnp.dot(tri, sp, preferred_element_type=F32)

    def head_queries(qi):
        qt = qt_ref[0, 0, qi]
        zero = jnp.zeros_like(qt)
        return (jnp.where(drow < HEAD_DIM, qt, zero), jnp.where(drow >= HEAD_DIM, qt, zero))

    def far_kb(qi, j):
        return jnp.maximum(2 * qi - 1 - j, 0)

    def stage_a(qi, slot):
        qh = head_queries(qi)
        k_top, k_lo = k_block(2 * qi + 1), k_block(2 * qi)
        k_far = [k_block(far_kb(qi, j)) for j in range(N_FAR)]
        for h in range(2):
            z0 = jnp.dot(k_top, qh[h][:, TK:], preferred_element_type=F32)
            z0_ref[slot, h] = z0
            sp0_ref[slot, h] = (softplus(z0) * mlo_ref[:, :TK]).astype(BF16)
            z1 = jnp.dot(k_lo, qh[h], preferred_element_type=F32)
            z_ref[slot, h, 0] = z1
            sp_ref[slot, h, 0] = (softplus(z1) * mlo_ref[...]).astype(BF16)
            for j in range(N_FAR):
                z = jnp.dot(k_far[j], qh[h], preferred_element_type=F32)
                z_ref[slot, h, 1 + j] = z
                sp_ref[slot, h, 1 + j] = softplus(z).astype(BF16)

    def stage_b1(slot):
        return [(cumsum(sp0_ref[slot, h]), [cumsum(sp_ref[slot, h, j]) for j in range(1 + N_FAR)])
                for h in range(2)]

    def stage_b2(qi, slot, sums):
        kbs = [2 * qi + 1, 2 * qi] + [far_kb(qi, j) for j in range(N_FAR)]
        vs = [vt_ref[0, 0, kb] for kb in kbs]
        no_far = jnp.where(qi == 0, NO_KEYS, 0.0).astype(F32)
        rmin, accs = None, []
        for h in range(2):
            c0, cs = sums[h]
            a0 = weights(z0_ref[slot, h], c0, mlo_ref[:, :TK])
            r = jnp.concatenate([jnp.zeros((1, TK), F32), c0[0:1, :]], axis=1)
            a1 = weights(z_ref[slot, h, 0], cs[0] + r, mlo_ref[...])
            r = r + cs[0][0:1, :] + no_far
            parts = [jnp.concatenate([jnp.zeros((TK, TK), BF16), a0], axis=1), a1]
            for j in range(N_FAR):
                c = cs[1 + j]
                parts.append(jnp.exp(z_ref[slot, h, 1 + j] - (c + r)).astype(BF16))
                r = r + c[0:1, :]
            hs = slice(h * HEAD_DIM, (h + 1) * HEAD_DIM)
            v_cat = jnp.concatenate([v[hs, :] for v in vs], axis=1)
            accs.append(jnp.dot(v_cat, jnp.concatenate(parts, axis=0),
                                preferred_element_type=F32))
            rall_ref[qi, h] = r
            m = jnp.min(r)
            rmin = m if rmin is None else jnp.minimum(rmin, m)
        rmin_ref[qi] = rmin
        write_tile(qi, jnp.concatenate(accs, axis=0).T)

    def far_block(kb, qh):
        kblk = k_block(kb)
        vblk = vt_ref[0, 0, kb]
        rmin = None
        zs = [jnp.dot(kblk, qh[h], preferred_element_type=F32) for h in range(2)]
        cs = [cumsum(softplus(z).astype(BF16)) for z in zs]
        for h in range(2):
            z, c = zs[h], cs[h]
            r_old = r_ref[h]
            a = jnp.exp(z - (c + r_old)).astype(BF16)
            acc_ref[h] += jnp.dot(vblk[h * HEAD_DIM:(h + 1) * HEAD_DIM, :], a,
                                  preferred_element_type=F32)
            r_new = r_old + c[0:1, :]
            r_ref[h] = r_new
            m = jnp.min(r_new)
            rmin = m if rmin is None else jnp.minimum(rmin, m)
        return rmin

    def finish(nq):
        def one_tile(qi, carry):
            kb0 = 2 * qi - 1 - N_FAR
            rmin0 = rmin_ref[qi]

            @pl.when(jnp.logical_and(kb0 >= 0, rmin0 < SKIP_THRESHOLD))
            def _():
                qh = head_queries(qi)
                acc_ref[...] = jnp.zeros_like(acc_ref)
                for h in range(2):
                    r_ref[h] = rall_ref[qi, h]

                def cond(c):
                    kb, rmin = c
                    return jnp.logical_and(kb >= 0, rmin < SKIP_THRESHOLD)

                def body(c):
                    kb, _ = c
                    return kb - 1, far_block(kb, qh)

                lax.while_loop(cond, body, (kb0, rmin0))
                acc = jnp.concatenate([acc_ref[0], acc_ref[1]], axis=0)
                add_tile(qi, acc.T)

            return carry

        lax.fori_loop(0, nq, one_tile, 0)

    return init, stage_a, stage_b1, stage_b2, finish


def _attn_mlp_kernel(qt_ref, k_ref, vt_ref, x_ref, ssm_ref, ga_ref, wout_ref, g2_ref, w1_ref, w2_ref,
                     o_ref, sb_ref, hn_ref, mlo_ref, z0_ref, sp0_ref, z_ref, sp_ref, acc_ref, r_ref,
                     rall_ref, rmin_ref):
    s, t = pl.program_id(0), pl.program_id(1)
    last = pl.num_programs(0) - 1
    nq = qt_ref.shape[2]
    wslot = lax.rem(s, 2)
    rslot = 1 - wslot

    def tile_rows(qi):
        return pl.ds(pl.multiple_of(qi * TQ, TQ), TQ)

    def write_tile(qi, tile):
        sb_ref[wslot, t, tile_rows(qi), :] = tile

    def add_tile(qi, tile):
        sb_ref[wslot, t, tile_rows(qi), :] += tile

    init, stage_a, stage_b1, stage_b2, finish = _attention_stages(
        qt_ref, k_ref, vt_ref, mlo_ref, z0_ref, sp0_ref, z_ref, sp_ref, acc_ref, r_ref, rall_ref,
        rmin_ref, write_tile, add_tile)

    def mlp_head():
        rows = pl.ds(pl.multiple_of(t * TM_MLP, TM_MLP), TM_MLP)
        sb = jnp.concatenate([sb_ref[rslot, p, rows, :] for p in range(HEAD_PAIRS)], axis=1)
        an = (_rms(sb) * ga_ref[...]).astype(BF16)
        h = x_ref[0] + jnp.dot(an, wout_ref[:SB_WIDTH, :], preferred_element_type=F32)
        h = h + jnp.dot(ssm_ref[0], wout_ref[SB_WIDTH:, :], preferred_element_type=F32)
        hn_ref[...] = (_rms(h) * g2_ref[...]).astype(BF16)
        o_ref[0] = h

    def mlp_chunk(c):
        a = jnp.dot(hn_ref[...], w1_ref[c], preferred_element_type=F32)
        a = jnp.square(jnp.maximum(a, 0.0)).astype(BF16)
        o_ref[0] += jnp.dot(a, w2_ref[c], preferred_element_type=F32)

    def step(attn, mlp):
        if attn:
            init()
            stage_a(0, 0)
        if mlp:
            mlp_head()

        def tile(qi, slot, next_slot):
            if attn:
                sums = stage_b1(slot)
                stage_a(jnp.minimum(qi + 1, nq - 1), next_slot)
            if mlp:
                mlp_chunk(qi)
            if attn:
                stage_b2(qi, slot, sums)

        def tile_pair(m, carry):
            tile(2 * m, 0, 1)
            tile(2 * m + 1, 1, 0)
            return carry

        lax.fori_loop(0, nq // 2, tile_pair, 0)
        if attn:
            finish(nq)

    @pl.when(s == 0)
    def _():
        step(True, False)

    @pl.when(jnp.logical_and(s > 0, s < last))
    def _():
        step(True, True)

    @pl.when(s == last)
    def _():
        step(False, True)


def _attn_mlp(qt, k, vt, x, ssm, ga, wout, g2, w1, w2):
    B, _, nq, _, _ = qt.shape
    L, D = x.shape[1], x.shape[2]
    nk = vt.shape[2]
    assert nq % 2 == 0 and nq >= 2 and nq == D_FF // FF_SUB and L // TM_MLP == HEAD_PAIRS
    const = lambda s, t: (0, 0)
    const3 = lambda s, t: (0, 0, 0)
    cur = lambda s: jnp.minimum(s, B - 1)
    prv = lambda s: jnp.maximum(s - 1, 0)
    tok = lambda s, t: jnp.where(s == 0, 0, t)
    return pl.pallas_call(
        _attn_mlp_kernel,
        grid=(B + 1, HEAD_PAIRS),
        in_specs=[
            pl.BlockSpec((1, 1, nq, LANES, TQ), lambda s, t: (cur(s), t, 0, 0, 0)),
            pl.BlockSpec((1, L, LANES), lambda s, t: (cur(s), 0, t)),
            pl.BlockSpec((1, 1, nk, LANES, TK), lambda s, t: (cur(s), t, 0, 0, 0)),
            pl.BlockSpec((1, TM_MLP, D), lambda s, t: (prv(s), tok(s, t), 0)),
            pl.BlockSpec((1, TM_MLP, SSM_WIDTH), lambda s, t: (prv(s), tok(s, t), 0)),
            pl.BlockSpec((1, SB_WIDTH), const),
            pl.BlockSpec((D, D), const),
            pl.BlockSpec((1, D), const),
            pl.BlockSpec((D_FF // FF_SUB, D, FF_SUB), const3),
            pl.BlockSpec((D_FF // FF_SUB, FF_SUB, D), const3),
        ],
        out_specs=pl.BlockSpec((1, TM_MLP, D), lambda s, t: (prv(s), tok(s, t), 0)),
        out_shape=jax.ShapeDtypeStruct((B, L, D), F32),
        scratch_shapes=[
            pltpu.VMEM((2, HEAD_PAIRS, L, LANES), F32),
            pltpu.VMEM((TM_MLP, D), BF16),
            pltpu.VMEM((TK, TQ), F32),
            pltpu.VMEM((2, 2, TK, TK), F32),
            pltpu.VMEM((2, 2, TK, TK), BF16),
            pltpu.VMEM((2, 2, 1 + N_FAR, TK, TQ), F32),
            pltpu.VMEM((2, 2, 1 + N_FAR, TK, TQ), BF16),
            pltpu.VMEM((2, HEAD_DIM, TQ), F32),
            pltpu.VMEM((2, 1, TQ), F32),
            pltpu.VMEM((nq, 2, 1, TQ), F32),
            pltpu.SMEM((nq,), F32),
        ],
        compiler_params=pltpu.CompilerParams(
            dimension_semantics=("arbitrary", "arbitrary"),
            vmem_limit_bytes=VMEM_LIMIT_BYTES),
        name="attn_mlp",
    )(qt, k, vt, x, ssm, ga, wout, g2, w1, w2)


def _s5_params(lam_re, lam_im, log_dt, b_re, b_im, c_re, c_im):
    G, P, H = SSM_GROUPS, SSM_STATE, SSM_GROUP
    lr, li = lam_re.astype(F32), lam_im.astype(F32)
    dt = jnp.exp(log_dt.astype(F32))[:, None]
    mag = jnp.exp(lr * dt)
    are, aim = mag * jnp.cos(li * dt), mag * jnp.sin(li * dt)
    den = lr * lr + li * li
    wr = ((are - 1.0) * lr + aim * li) / den
    wi = (aim * lr - (are - 1.0) * li) / den
    bbr = wr[:, :, None] * b_re.astype(F32) - wi[:, :, None] * b_im.astype(F32)
    bbi = wr[:, :, None] * b_im.astype(F32) + wi[:, :, None] * b_re.astype(F32)
    are, aim = are.reshape(1, G * P), aim.reshape(1, G * P)
    gpb = MXU_DIM // H
    eye = jnp.eye(gpb, dtype=F32)

    def b_layout(b):
        b = b.reshape(G // gpb, gpb, P, H)
        return jnp.einsum("kgph,gf->kghfp", b, eye).reshape(G // gpb, gpb * H, gpb * P).astype(BF16)

    def c_layout(c):
        c = c.reshape(G // gpb, gpb, H, P)
        return jnp.einsum("kghp,gf->kgpfh", c, eye).reshape(G // gpb, gpb * P, gpb * H).astype(BF16)

    return (b_layout(bbr), b_layout(bbi), are, aim,
            c_layout(c_re.astype(F32)), c_layout(-c_im.astype(F32)))


def kernel(x, norm1_g, w_in, q_norm_g, k_norm_g, ssm_lambda_re, ssm_lambda_im, ssm_log_dt,
           ssm_b_re, ssm_b_im, ssm_c_re, ssm_c_im, ssm_d, w_glu, b_glu, attn_out_g,
           ssm_out_g, w_out, norm2_g, w_mlp_in, w_mlp_out):
    B, L, D = x.shape
    assert (D, L % TM_PROJ, L % TM_MLP, L % TT) == (D_MODEL, 0, 0, 0)
    heads = SB_WIDTH // HEAD_DIM
    row = lambda g: g.astype(F32).reshape(1, -1)

    head_id = jnp.arange(SB_WIDTH) // HEAD_DIM
    hsel = (head_id[:, None] == head_id[None, :]).astype(BF16) * (1.0 / HEAD_DIM)
    gq = row(jnp.tile(q_norm_g.astype(F32), heads)) * (1.0 / math.sqrt(HEAD_DIM))
    gk = row(jnp.tile(k_norm_g.astype(F32), heads))

    qt, k, vt, u = _inproj(x.astype(F32), row(norm1_g), w_in.astype(BF16), hsel, gq, gk)

    bre, bim, are, aim, cre, cim = _s5_params(
        ssm_lambda_re, ssm_lambda_im, ssm_log_dt, ssm_b_re, ssm_b_im, ssm_c_re, ssm_c_im)
    ssm = _s5(u, bre, bim, are, aim, cre, cim,
              row(ssm_d), w_glu.astype(BF16), row(b_glu), row(ssm_out_g))

    chunks = D_FF // FF_SUB
    w1 = w_mlp_in.astype(BF16).reshape(D, chunks, FF_SUB).transpose(1, 0, 2)
    w2 = w_mlp_out.astype(BF16).reshape(chunks, FF_SUB, D)
    out = _attn_mlp(qt, k, vt, x.astype(F32), ssm, row(attn_out_g), w_out.astype(BF16),
                    row(norm2_g), w1, w2)
    return out.astype(x.dtype)
```

```python
import functools
import math

import jax
import jax.numpy as jnp
from jax import lax
from jax.experimental import pallas as pl
from jax.experimental.pallas import tpu as pltpu

F32 = jnp.float32
BF16 = jnp.bfloat16

D_MODEL = 1024
SB_WIDTH = 512
HEAD_DIM = 64
HEAD_PAIRS = SB_WIDTH // (2 * HEAD_DIM)
SSM_WIDTH = 512
SSM_GROUP = 16
SSM_GROUPS = 32
SSM_STATE = 64
SSM_REAL = SSM_GROUPS * SSM_STATE
D_FF = 4 * D_MODEL
EPS = 1e-6

LANES = 128
MXU_DIM = 256
VMEM_LIMIT_BYTES = 56 * 1024 * 1024

TM_PROJ = 512
TQ = 256
TK = 128
TT = 32
SCAN_W = 512
TM_MLP = 512
FF_SUB = 512

SKIP_THRESHOLD = 88.0
N_FAR = 1
NO_KEYS = 1e30


def _rms(x):
    return x * lax.rsqrt(jnp.mean(x * x, axis=-1, keepdims=True) + EPS)


def _inproj_kernel(x_ref, g1_ref, win_ref, hsel_ref, gq_ref, gk_ref,
                   qt_ref, k_ref, vt_ref, u_ref):
    x = x_ref[0]
    xn = (_rms(x) * g1_ref[...]).astype(BF16)
    proj = jnp.dot(xn, win_ref[...], preferred_element_type=F32)
    q = proj[:, 0 * SB_WIDTH:1 * SB_WIDTH]
    k = proj[:, 1 * SB_WIDTH:2 * SB_WIDTH]
    v = proj[:, 2 * SB_WIDTH:3 * SB_WIDTH]
    u = proj[:, 3 * SB_WIDTH:]
    hsel = hsel_ref[...]
    msq = jnp.dot((q * q).astype(BF16), hsel, preferred_element_type=F32)
    msk = jnp.dot((k * k).astype(BF16), hsel, preferred_element_type=F32)
    qn = q * lax.rsqrt(msq + EPS) * gq_ref[...]
    kn = k * lax.rsqrt(msk + EPS) * gk_ref[...]
    k_ref[0] = kn.astype(BF16)
    u_ref[0] = u.astype(BF16)
    for j in range(TM_PROJ // TQ):
        qt = qn[j * TQ:(j + 1) * TQ, :].T
        qt_ref[0, :, j] = qt.astype(BF16).reshape(HEAD_PAIRS, LANES, TQ)
    for j in range(TM_PROJ // TK):
        vt = v[j * TK:(j + 1) * TK, :].T
        vt_ref[0, :, j] = vt.astype(BF16).reshape(HEAD_PAIRS, LANES, TK)


def _inproj(x, g1, win, hsel, gq, gk):
    B, L, D = x.shape
    nq, nk = L // TQ, L // TK
    const = lambda b, t: (0, 0)
    return pl.pallas_call(
        _inproj_kernel,
        grid=(B, L // TM_PROJ),
        in_specs=[
            pl.BlockSpec((1, TM_PROJ, D), lambda b, t: (b, t, 0)),
            pl.BlockSpec((1, D), const),
            pl.BlockSpec((D, 4 * SB_WIDTH), const),
            pl.BlockSpec((SB_WIDTH, SB_WIDTH), const),
            pl.BlockSpec((1, SB_WIDTH), const),
            pl.BlockSpec((1, SB_WIDTH), const),
        ],
        out_specs=[
            pl.BlockSpec((1, HEAD_PAIRS, TM_PROJ // TQ, LANES, TQ), lambda b, t: (b, 0, t, 0, 0)),
            pl.BlockSpec((1, TM_PROJ, SB_WIDTH), lambda b, t: (b, t, 0)),
            pl.BlockSpec((1, HEAD_PAIRS, TM_PROJ // TK, LANES, TK), lambda b, t: (b, 0, t, 0, 0)),
            pl.BlockSpec((1, TM_PROJ, SSM_WIDTH), lambda b, t: (b, t, 0)),
        ],
        out_shape=[
            jax.ShapeDtypeStruct((B, HEAD_PAIRS, nq, LANES, TQ), BF16),
            jax.ShapeDtypeStruct((B, L, SB_WIDTH), BF16),
            jax.ShapeDtypeStruct((B, HEAD_PAIRS, nk, LANES, TK), BF16),
            jax.ShapeDtypeStruct((B, L, SSM_WIDTH), BF16),
        ],
        compiler_params=pltpu.CompilerParams(
            dimension_semantics=("parallel", "parallel"),
            vmem_limit_bytes=VMEM_LIMIT_BYTES),
        name="inproj",
    )(x, g1, win, hsel, gq, gk)


def _gelu_tanh(x):
    c = math.sqrt(2.0 / math.pi)
    return 0.5 * x * (1.0 + jnp.tanh(c * (x + 0.044715 * (x * x * x))))


def _s5_kernel(u_ref, perm_ref, permt_ref, bre_ref, bim_ref, are_ref, aim_ref, cre_ref, cim_ref,
               d_ref, wglu_ref, bglu_ref, gs_ref, o_ref, ub_ref, bu_ref, xs_ref, st_ref, *, batch):
    s = pl.program_id(0)

    @pl.when(s == 0)
    def _():
        for ref in (ub_ref, bu_ref, xs_ref, st_ref):
            ref[...] = jnp.zeros_like(ref)

    rows = batch * TT
    half = SSM_REAL // 2
    cur = lax.rem(s, 2)
    prev = 1 - cur

    u = jnp.dot(perm_ref[...], u_ref[...].reshape(rows, SSM_WIDTH),
                preferred_element_type=F32).astype(BF16)
    ub_ref[lax.rem(s, 3)] = u
    for kb in range(2):
        ub = u[:, kb * MXU_DIM:(kb + 1) * MXU_DIM]
        bu_ref[cur, :, kb * half:(kb + 1) * half] = jnp.dot(
            ub, bre_ref[kb], preferred_element_type=F32)
        bu_ref[cur, :, SSM_REAL + kb * half:SSM_REAL + (kb + 1) * half] = jnp.dot(
            ub, bim_ref[kb], preferred_element_type=F32)

    ys = []
    for ob in range(2):
        y = jnp.dot(xs_ref[cur, :, ob * half:(ob + 1) * half], cre_ref[ob],
                    preferred_element_type=F32)
        y = y + jnp.dot(xs_ref[cur, :, SSM_REAL + ob * half:SSM_REAL + (ob + 1) * half],
                        cim_ref[ob], preferred_element_type=F32)
        ys.append(y)
    y = jnp.concatenate(ys, axis=1) + d_ref[...] * ub_ref[lax.rem(s + 1, 3)].astype(F32)
    y = _gelu_tanh(y)
    gate = jax.nn.sigmoid(
        jnp.dot(y.astype(BF16), wglu_ref[...], preferred_element_type=F32) + bglu_ref[...])
    out_tb = (_rms(y * gate) * gs_ref[...]).astype(BF16)
    out_bt = jnp.dot(permt_ref[...], out_tb, preferred_element_type=F32)
    o_ref[...] = out_bt.astype(o_ref.dtype).reshape(batch, TT, SSM_WIDTH)

    for sc in range(SSM_REAL // SCAN_W):
        re = slice(sc * SCAN_W, (sc + 1) * SCAN_W)
        im = slice(SSM_REAL + sc * SCAN_W, SSM_REAL + (sc + 1) * SCAN_W)
        ar = jnp.broadcast_to(are_ref[:, re], (batch, SCAN_W))
        ai = jnp.broadcast_to(aim_ref[:, re], (batch, SCAN_W))

        def step(t, carry, re=re, im=im, ar=ar, ai=ai):
            xr, xi = carry
            rs = pl.ds(pl.multiple_of(t * batch, batch), batch)
            nxr = ar * xr - ai * xi + bu_ref[prev, rs, re]
            nxi = ar * xi + ai * xr + bu_ref[prev, rs, im]
            xs_ref[prev, rs, re] = nxr.astype(BF16)
            xs_ref[prev, rs, im] = nxi.astype(BF16)
            return nxr, nxi

        xr, xi = lax.fori_loop(0, TT, step, (st_ref[:, re], st_ref[:, im]), unroll=4)
        st_ref[:, re] = xr
        st_ref[:, im] = xi


def _s5(u, bre, bim, are, aim, cre, cim, d, wglu, bglu, gs):
    batch, L, _ = u.shape
    rows = TT * batch
    tiles = L // TT
    half = SSM_REAL // 2
    c2 = lambda s: (0, 0)
    c3 = lambda s: (0, 0, 0)
    src = (jnp.arange(rows) % batch) * TT + jnp.arange(rows) // batch
    perm = (src[:, None] == jnp.arange(rows)[None, :]).astype(BF16)
    return pl.pallas_call(
        functools.partial(_s5_kernel, batch=batch),
        grid=(tiles + 2,),
        in_specs=[
            pl.BlockSpec((batch, TT, SSM_WIDTH), lambda s: (0, jnp.minimum(s, tiles - 1), 0)),
            pl.BlockSpec((rows, rows), c2),
            pl.BlockSpec((rows, rows), c2),
            pl.BlockSpec((2, MXU_DIM, half), c3),
            pl.BlockSpec((2, MXU_DIM, half), c3),
            pl.BlockSpec((1, SSM_REAL), c2),
            pl.BlockSpec((1, SSM_REAL), c2),
            pl.BlockSpec((2, half, MXU_DIM), c3),
            pl.BlockSpec((2, half, MXU_DIM), c3),
            pl.BlockSpec((1, SSM_WIDTH), c2),
            pl.BlockSpec((SSM_WIDTH, SSM_WIDTH), c2),
            pl.BlockSpec((1, SSM_WIDTH), c2),
            pl.BlockSpec((1, SSM_WIDTH), c2),
        ],
        out_specs=pl.BlockSpec((batch, TT, SSM_WIDTH), lambda s: (0, jnp.maximum(s - 2, 0), 0)),
        out_shape=jax.ShapeDtypeStruct((batch, L, SSM_WIDTH), BF16),
        scratch_shapes=[
            pltpu.VMEM((3, rows, SSM_WIDTH), BF16),
            pltpu.VMEM((2, rows, 2 * SSM_REAL), F32),
            pltpu.VMEM((2, rows, 2 * SSM_REAL), BF16),
            pltpu.VMEM((batch, 2 * SSM_REAL), F32),
        ],
        compiler_params=pltpu.CompilerParams(
            dimension_semantics=("arbitrary",),
            vmem_limit_bytes=VMEM_LIMIT_BYTES),
        name="s5_glu",
    )(u, perm, perm.T, bre, bim, are, aim, cre, cim, d, wglu, bglu, gs)


def _attention_stages(qt_ref, k_ref, vt_ref, mlo_ref, z0_ref, sp0_ref, z_ref, sp_ref, acc_ref,
                      r_ref, rall_ref, rmin_ref, write_tile, add_tile):
    srow = lax.broadcasted_iota(jnp.int32, (TK, TK), 0)
    scol = lax.broadcasted_iota(jnp.int32, (TK, TK), 1)
    tri = jnp.where(scol >= srow, 1.0, 0.0).astype(BF16)
    drow = lax.broadcasted_iota(jnp.int32, (LANES, TQ), 0)

    def init():
        row = lax.broadcasted_iota(jnp.int32, (TK, TQ), 0)
        col = lax.broadcasted_iota(jnp.int32, (TK, TQ), 1)
        mlo_ref[...] = jnp.where(row < col, 1.0, 0.0)

    def weights(z, s, mask):
        return (jnp.exp(jnp.minimum(z - s, 0.0)) * mask).astype(BF16)

    def k_block(kb):
        return k_ref[0, pl.ds(pl.multiple_of(kb * TK, TK), TK), :]

    def softplus(z):
        return jnp.maximum(z, 0.0) + jnp.log(1.0 + jnp.exp(-jnp.abs(z)))

    def cumsum(sp):
        return jnp.dot(tri, sp, preferred_element_type=F32)

    def head_queries(qi):
        qt = qt_ref[0, 0, qi]
        zero = jnp.zeros_like(qt)
        return (jnp.where(drow < HEAD_DIM, qt, zero), jnp.where(drow >= HEAD_DIM, qt, zero))

    def far_kb(qi, j):
        return jnp.maximum(2 * qi - 1 - j, 0)

    def stage_a(qi, slot):
        qh = head_queries(qi)
        k_top, k_lo = k_block(2 * qi + 1), k_block(2 * qi)
        k_far = [k_block(far_kb(qi, j)) for j in range(N_FAR)]
        for h in range(2):
            z0 = jnp.dot(k_top, qh[h][:, TK:], preferred_element_type=F32)
            z0_ref[slot, h] = z0
            sp0_ref[slot, h] = (softplus(z0) * mlo_ref[:, :TK]).astype(BF16)
            z1 = jnp.dot(k_lo, qh[h], preferred_element_type=F32)
            z_ref[slot, h, 0] = z1
            sp_ref[slot, h, 0] = (softplus(z1) * mlo_ref[...]).astype(BF16)
            for j in range(N_FAR):
                z = jnp.dot(k_far[j], qh[h], preferred_element_type=F32)
                z_ref[slot, h, 1 + j] = z
                sp_ref[slot, h, 1 + j] = softplus(z).astype(BF16)

    def stage_b1(slot):
        return [(cumsum(sp0_ref[slot, h]), [cumsum(sp_ref[slot, h, j]) for j in range(1 + N_FAR)])
                for h in range(2)]

    def stage_b2(qi, slot, sums):
        kbs = [2 * qi + 1, 2 * qi] + [far_kb(qi, j) for j in range(N_FAR)]
        vs = [vt_ref[0, 0, kb] for kb in kbs]
        no_far = jnp.where(qi == 0, NO_KEYS, 0.0).astype(F32)
        rmin, accs = None, []
        for h in range(2):
            c0, cs = sums[h]
            a0 = weights(z0_ref[slot, h], c0, mlo_ref[:, :TK])
            r = jnp.concatenate([jnp.zeros((1, TK), F32), c0[0:1, :]], axis=1)
            a1 = weights(z_ref[slot, h, 0], cs[0] + r, mlo_ref[...])
            r = r + cs[0][0:1, :] + no_far
            parts = [jnp.concatenate([jnp.zeros((TK, TK), BF16), a0], axis=1), a1]
            for j in range(N_FAR):
                c = cs[1 + j]
                parts.append(jnp.exp(z_ref[slot, h, 1 + j] - (c + r)).astype(BF16))
                r = r + c[0:1, :]
            hs = slice(h * HEAD_DIM, (h + 1) * HEAD_DIM)
            v_cat = jnp.concatenate([v[hs, :] for v in vs], axis=1)
            accs.append(jnp.dot(v_cat, jnp.concatenate(parts, axis=0),
                                preferred_element_type=F32))
            rall_ref[qi, h] = r
            m = jnp.min(r)
            rmin = m if rmin is None else jnp.minimum(rmin, m)
        rmin_ref[qi] = rmin
        write_tile(qi, jnp.concatenate(accs, axis=0).T)

    def far_block(kb, qh):
        kblk = k_block(kb)
        vblk = vt_ref[0, 0, kb]
        rmin = None
        zs = [jnp.dot(kblk, qh[h], preferred_element_type=F32) for h in range(2)]
        cs = [cumsum(softplus(z).astype(BF16)) for z in zs]
        for h in range(2):
            z, c = zs[h], cs[h]
            r_old = r_ref[h]
            a = jnp.exp(z - (c + r_old)).astype(BF16)
            acc_ref[h] += jnp.dot(vblk[h * HEAD_DIM:(h + 1) * HEAD_DIM, :], a,
                                  preferred_element_type=F32)
            r_new = r_old + c[0:1, :]
            r_ref[h] = r_new
            m = jnp.min(r_new)
            rmin = m if rmin is None else jnp.minimum(rmin, m)
        return rmin

    def finish(nq):
        def one_tile(qi, carry):
            kb0 = 2 * qi - 1 - N_FAR
            rmin0 = rmin_ref[qi]

            @pl.when(jnp.logical_and(kb0 >= 0, rmin0 < SKIP_THRESHOLD))
            def _():
                qh = head_queries(qi)
                acc_ref[...] = jnp.zeros_like(acc_ref)
                for h in range(2):
                    r_ref[h] = rall_ref[qi, h]

                def cond(c):
                    kb, rmin = c
                    return jnp.logical_and(kb >= 0, rmin < SKIP_THRESHOLD)

                def body(c):
                    kb, _ = c
                    return kb - 1, far_block(kb, qh)

                lax.while_loop(cond, body, (kb0, rmin0))
                acc = jnp.concatenate([acc_ref[0], acc_ref[1]], axis=0)
                add_tile(qi, acc.T)

            return carry

        lax.fori_loop(0, nq, one_tile, 0)

    return init, stage_a, stage_b1, stage_b2, finish


def _attn_mlp_kernel(qt_ref, k_ref, vt_ref, x_ref, ssm_ref, ga_ref, wout_ref, g2_ref, w1_ref, w2_ref,
                     o_ref, sb_ref, hn_ref, mlo_ref, z0_ref, sp0_ref, z_ref, sp_ref, acc_ref, r_ref,
                     rall_ref, rmin_ref):
    s, t = pl.program_id(0), pl.program_id(1)
    last = pl.num_programs(0) - 1
    nq = qt_ref.shape[2]
    wslot = lax.rem(s, 2)
    rslot = 1 - wslot

    def tile_rows(qi):
        return pl.ds(pl.multiple_of(qi * TQ, TQ), TQ)

    def write_tile(qi, tile):
        sb_ref[wslot, t, tile_rows(qi), :] = tile

    def add_tile(qi, tile):
        sb_ref[wslot, t, tile_rows(qi), :] += tile

    init, stage_a, stage_b1, stage_b2, finish = _attention_stages(
        qt_ref, k_ref, vt_ref, mlo_ref, z0_ref, sp0_ref, z_ref, sp_ref, acc_ref, r_ref, rall_ref,
        rmin_ref, write_tile, add_tile)

    def mlp_head():
        rows = pl.ds(pl.multiple_of(t * TM_MLP, TM_MLP), TM_MLP)
        sb = jnp.concatenate([sb_ref[rslot, p, rows, :] for p in range(HEAD_PAIRS)], axis=1)
        an = (_rms(sb) * ga_ref[...]).astype(BF16)
        h = x_ref[0] + jnp.dot(an, wout_ref[:SB_WIDTH, :], preferred_element_type=F32)
        h = h + jnp.dot(ssm_ref[0], wout_ref[SB_WIDTH:, :], preferred_element_type=F32)
        hn_ref[...] = (_rms(h) * g2_ref[...]).astype(BF16)
        o_ref[0] = h

    def mlp_chunk(c):
        cols = slice(c * FF_SUB, (c + 1) * FF_SUB)
        a = jnp.dot(hn_ref[...], w1_ref[:, cols], preferred_element_type=F32)
        a = jnp.square(jnp.maximum(a, 0.0)).astype(BF16)
        o_ref[0] += jnp.dot(a, w2_ref[cols, :], preferred_element_type=F32)

    def step(attn, mlp):
        if attn:
            init()
            stage_a(0, 0)
        if mlp:
            mlp_head()

        def tile(qi, slot, next_slot, has_next=True):
            if attn:
                sums = stage_b1(slot)
                if has_next:
                    stage_a(qi + 1, next_slot)
            if mlp:
                mlp_chunk(qi)
            if attn:
                stage_b2(qi, slot, sums)

        for qi in range(nq):
            tile(qi, qi % 2, 1 - qi % 2, has_next=qi + 1 < nq)
        if attn:
            finish(nq)

    @pl.when(s == 0)
    def _():
        step(True, False)

    @pl.when(jnp.logical_and(s > 0, s < last))
    def _():
        step(True, True)

    @pl.when(s == last)
    def _():
        step(False, True)


def _attn_mlp(qt, k, vt, x, ssm, ga, wout, g2, w1, w2):
    B, _, nq, _, _ = qt.shape
    L, D = x.shape[1], x.shape[2]
    nk = vt.shape[2]
    assert nq % 2 == 0 and nq >= 2 and nq == D_FF // FF_SUB and L // TM_MLP == HEAD_PAIRS
    const = lambda s, t: (0, 0)
    cur = lambda s: jnp.minimum(s, B - 1)
    prv = lambda s: jnp.maximum(s - 1, 0)
    tok = lambda s, t: jnp.where(s == 0, 0, t)
    return pl.pallas_call(
        _attn_mlp_kernel,
        grid=(B + 1, HEAD_PAIRS),
        in_specs=[
            pl.BlockSpec((1, 1, nq, LANES, TQ), lambda s, t: (cur(s), t, 0, 0, 0)),
            pl.BlockSpec((1, L, LANES), lambda s, t: (cur(s), 0, t)),
            pl.BlockSpec((1, 1, nk, LANES, TK), lambda s, t: (cur(s), t, 0, 0, 0)),
            pl.BlockSpec((1, TM_MLP, D), lambda s, t: (prv(s), tok(s, t), 0)),
            pl.BlockSpec((1, TM_MLP, SSM_WIDTH), lambda s, t: (prv(s), tok(s, t), 0)),
            pl.BlockSpec((1, SB_WIDTH), const),
            pl.BlockSpec((D, D), const),
            pl.BlockSpec((1, D), const),
            pl.BlockSpec((D, D_FF), const),
            pl.BlockSpec((D_FF, D), const),
        ],
        out_specs=pl.BlockSpec((1, TM_MLP, D), lambda s, t: (prv(s), tok(s, t), 0)),
        out_shape=jax.ShapeDtypeStruct((B, L, D), F32),
        scratch_shapes=[
            pltpu.VMEM((2, HEAD_PAIRS, L, LANES), F32),
            pltpu.VMEM((TM_MLP, D), BF16),
            pltpu.VMEM((TK, TQ), F32),
            pltpu.VMEM((2, 2, TK, TK), F32),
            pltpu.VMEM((2, 2, TK, TK), BF16),
            pltpu.VMEM((2, 2, 1 + N_FAR, TK, TQ), F32),
            pltpu.VMEM((2, 2, 1 + N_FAR, TK, TQ), BF16),
            pltpu.VMEM((2, HEAD_DIM, TQ), F32),
            pltpu.VMEM((2, 1, TQ), F32),
            pltpu.VMEM((nq, 2, 1, TQ), F32),
            pltpu.SMEM((nq,), F32),
        ],
        compiler_params=pltpu.CompilerParams(
            dimension_semantics=("arbitrary", "arbitrary"),
            vmem_limit_bytes=VMEM_LIMIT_BYTES),
        name="attn_mlp",
    )(qt, k, vt, x, ssm, ga, wout, g2, w1, w2)


def _s5_params(lam_re, lam_im, log_dt, b_re, b_im, c_re, c_im):
    G, P, H = SSM_GROUPS, SSM_STATE, SSM_GROUP
    lr, li = lam_re.astype(F32), lam_im.astype(F32)
    dt = jnp.exp(log_dt.astype(F32))[:, None]
    mag = jnp.exp(lr * dt)
    are, aim = mag * jnp.cos(li * dt), mag * jnp.sin(li * dt)
    den = lr * lr + li * li
    wr = ((are - 1.0) * lr + aim * li) / den
    wi = (aim * lr - (are - 1.0) * li) / den
    bbr = wr[:, :, None] * b_re.astype(F32) - wi[:, :, None] * b_im.astype(F32)
    bbi = wr[:, :, None] * b_im.astype(F32) + wi[:, :, None] * b_re.astype(F32)
    are, aim = are.reshape(1, G * P), aim.reshape(1, G * P)
    gpb = MXU_DIM // H
    eye = jnp.eye(gpb, dtype=F32)

    def b_layout(b):
        b = b.reshape(G // gpb, gpb, P, H)
        return jnp.einsum("kgph,gf->kghfp", b, eye).reshape(G // gpb, gpb * H, gpb * P).astype(BF16)

    def c_layout(c):
        c = c.reshape(G // gpb, gpb, H, P)
        return jnp.einsum("kghp,gf->kgpfh", c, eye).reshape(G // gpb, gpb * P, gpb * H).astype(BF16)

    return (b_layout(bbr), b_layout(bbi), are, aim,
            c_layout(c_re.astype(F32)), c_layout(-c_im.astype(F32)))


def kernel(x, norm1_g, w_in, q_norm_g, k_norm_g, ssm_lambda_re, ssm_lambda_im, ssm_log_dt,
           ssm_b_re, ssm_b_im, ssm_c_re, ssm_c_im, ssm_d, w_glu, b_glu, attn_out_g,
           ssm_out_g, w_out, norm2_g, w_mlp_in, w_mlp_out):
    B, L, D = x.shape
    assert (D, L % TM_PROJ, L % TM_MLP, L % TT) == (D_MODEL, 0, 0, 0)
    heads = SB_WIDTH // HEAD_DIM
    row = lambda g: g.astype(F32).reshape(1, -1)

    head_id = jnp.arange(SB_WIDTH) // HEAD_DIM
    hsel = (head_id[:, None] == head_id[None, :]).astype(BF16) * (1.0 / HEAD_DIM)
    gq = row(jnp.tile(q_norm_g.astype(F32), heads)) * (1.0 / math.sqrt(HEAD_DIM))
    gk = row(jnp.tile(k_norm_g.astype(F32), heads))

    qt, k, vt, u = _inproj(x.astype(F32), row(norm1_g), w_in.astype(BF16), hsel, gq, gk)

    bre, bim, are, aim, cre, cim = _s5_params(
        ssm_lambda_re, ssm_lambda_im, ssm_log_dt, ssm_b_re, ssm_b_im, ssm_c_re, ssm_c_im)
    ssm = _s5(u, bre, bim, are, aim, cre, cim,
              row(ssm_d), w_glu.astype(BF16), row(b_glu), row(ssm_out_g))

    out = _attn_mlp(qt, k, vt, x.astype(F32), ssm, row(attn_out_g), w_out.astype(BF16),
                    row(norm2_g), w_mlp_in.astype(BF16), w_mlp_out.astype(BF16))
    return out.astype(x.dtype)
```

```python
import functools
import math

import jax
import jax.numpy as jnp
from jax import lax
from jax.experimental import pallas as pl
from jax.experimental.pallas import tpu as pltpu

F32 = jnp.float32
BF16 = jnp.bfloat16

D_MODEL = 1024
SB_WIDTH = 512
HEAD_DIM = 64
HEAD_PAIRS = SB_WIDTH // (2 * HEAD_DIM)
SSM_WIDTH = 512
SSM_GROUP = 16
SSM_GROUPS = 32
SSM_STATE = 64
SSM_REAL = SSM_GROUPS * SSM_STATE
D_FF = 4 * D_MODEL
EPS = 1e-6

LANES = 128
MXU_DIM = 256
VMEM_LIMIT_BYTES = 56 * 1024 * 1024

TM_PROJ = 512
TQ = 256
TK = 128
TT = 32
SCAN_W = 512
TM_MLP = 512
FF_SUB = 512

SKIP_THRESHOLD = 88.0
N_FAR = 1
NO_KEYS = 1e30


def _rms(x):
    return x * lax.rsqrt(jnp.mean(x * x, axis=-1, keepdims=True) + EPS)


def _inproj_kernel(x_ref, g1_ref, win_ref, hsel_ref, gq_ref, gk_ref,
                   qt_ref, k_ref, vt_ref, u_ref):
    x = x_ref[0]
    xn = (_rms(x) * g1_ref[...]).astype(BF16)
    proj = jnp.dot(xn, win_ref[...], preferred_element_type=F32)
    q = proj[:, 0 * SB_WIDTH:1 * SB_WIDTH]
    k = proj[:, 1 * SB_WIDTH:2 * SB_WIDTH]
    v = proj[:, 2 * SB_WIDTH:3 * SB_WIDTH]
    u = proj[:, 3 * SB_WIDTH:]
    hsel = hsel_ref[...]
    msq = jnp.dot((q * q).astype(BF16), hsel, preferred_element_type=F32)
    msk = jnp.dot((k * k).astype(BF16), hsel, preferred_element_type=F32)
    qn = q * lax.rsqrt(msq + EPS) * gq_ref[...]
    kn = k * lax.rsqrt(msk + EPS) * gk_ref[...]
    k_ref[0] = kn.astype(BF16)
    u_ref[0] = u.astype(BF16)
    for j in range(TM_PROJ // TQ):
        qt = qn[j * TQ:(j + 1) * TQ, :].T
        qt_ref[0, :, j] = qt.astype(BF16).reshape(HEAD_PAIRS, LANES, TQ)
    for j in range(TM_PROJ // TK):
        vt = v[j * TK:(j + 1) * TK, :].T
        vt_ref[0, :, j] = vt.astype(BF16).reshape(HEAD_PAIRS, LANES, TK)


def _inproj(x, g1, win, hsel, gq, gk):
    B, L, D = x.shape
    nq, nk = L // TQ, L // TK
    const = lambda b, t: (0, 0)
    return pl.pallas_call(
        _inproj_kernel,
        grid=(B, L // TM_PROJ),
        in_specs=[
            pl.BlockSpec((1, TM_PROJ, D), lambda b, t: (b, t, 0)),
            pl.BlockSpec((1, D), const),
            pl.BlockSpec((D, 4 * SB_WIDTH), const),
            pl.BlockSpec((SB_WIDTH, SB_WIDTH), const),
            pl.BlockSpec((1, SB_WIDTH), const),
            pl.BlockSpec((1, SB_WIDTH), const),
        ],
        out_specs=[
            pl.BlockSpec((1, HEAD_PAIRS, TM_PROJ // TQ, LANES, TQ), lambda b, t: (b, 0, t, 0, 0)),
            pl.BlockSpec((1, TM_PROJ, SB_WIDTH), lambda b, t: (b, t, 0)),
            pl.BlockSpec((1, HEAD_PAIRS, TM_PROJ // TK, LANES, TK), lambda b, t: (b, 0, t, 0, 0)),
            pl.BlockSpec((1, TM_PROJ, SSM_WIDTH), lambda b, t: (b, t, 0)),
        ],
        out_shape=[
            jax.ShapeDtypeStruct((B, HEAD_PAIRS, nq, LANES, TQ), BF16),
            jax.ShapeDtypeStruct((B, L, SB_WIDTH), BF16),
            jax.ShapeDtypeStruct((B, HEAD_PAIRS, nk, LANES, TK), BF16),
            jax.ShapeDtypeStruct((B, L, SSM_WIDTH), BF16),
        ],
        compiler_params=pltpu.CompilerParams(
            dimension_semantics=("parallel", "parallel"),
            vmem_limit_bytes=VMEM_LIMIT_BYTES),
        name="inproj",
    )(x, g1, win, hsel, gq, gk)


def _gelu_tanh(x):
    c = math.sqrt(2.0 / math.pi)
    return 0.5 * x * (1.0 + jnp.tanh(c * (x + 0.044715 * (x * x * x))))


def _s5_kernel(u_ref, perm_ref, permt_ref, bre_ref, bim_ref, are_ref, aim_ref, cre_ref, cim_ref,
               d_ref, wglu_ref, bglu_ref, gs_ref, o_ref, ub_ref, bu_ref, xs_ref, st_ref, *, batch):
    s = pl.program_id(0)

    @pl.when(s == 0)
    def _():
        for ref in (ub_ref, bu_ref, xs_ref, st_ref):
            ref[...] = jnp.zeros_like(ref)

    rows = batch * TT
    half = SSM_REAL // 2
    cur = lax.rem(s, 2)
    prev = 1 - cur

    def stage1_b_bar_u(u, kb):
        ub = u[:, kb * MXU_DIM:(kb + 1) * MXU_DIM]
        bu_ref[cur, :, kb * half:(kb + 1) * half] = jnp.dot(
            ub, bre_ref[kb], preferred_element_type=F32)
        bu_ref[cur, :, SSM_REAL + kb * half:SSM_REAL + (kb + 1) * half] = jnp.dot(
            ub, bim_ref[kb], preferred_element_type=F32)

    ys = []
    for ob in range(2):
        y = jnp.dot(xs_ref[cur, :, ob * half:(ob + 1) * half], cre_ref[ob],
                    preferred_element_type=F32)
        y = y + jnp.dot(xs_ref[cur, :, SSM_REAL + ob * half:SSM_REAL + (ob + 1) * half],
                        cim_ref[ob], preferred_element_type=F32)
        ys.append(y)
    y = jnp.concatenate(ys, axis=1) + d_ref[...] * ub_ref[lax.rem(s + 1, 3)].astype(F32)

    u = jnp.dot(perm_ref[...], u_ref[...].reshape(rows, SSM_WIDTH),
                preferred_element_type=F32).astype(BF16)
    ub_ref[lax.rem(s, 3)] = u
    stage1_b_bar_u(u, 0)

    y = _gelu_tanh(y)
    gate = jax.nn.sigmoid(
        jnp.dot(y.astype(BF16), wglu_ref[...], preferred_element_type=F32) + bglu_ref[...])
    stage1_b_bar_u(u, 1)

    out_tb = (_rms(y * gate) * gs_ref[...]).astype(BF16)
    out_bt = jnp.dot(permt_ref[...], out_tb, preferred_element_type=F32)
    o_ref[...] = out_bt.astype(o_ref.dtype).reshape(batch, TT, SSM_WIDTH)

    for sc in range(SSM_REAL // SCAN_W):
        re = slice(sc * SCAN_W, (sc + 1) * SCAN_W)
        im = slice(SSM_REAL + sc * SCAN_W, SSM_REAL + (sc + 1) * SCAN_W)
        ar = jnp.broadcast_to(are_ref[:, re], (batch, SCAN_W))
        ai = jnp.broadcast_to(aim_ref[:, re], (batch, SCAN_W))

        def step(t, carry, re=re, im=im, ar=ar, ai=ai):
            xr, xi = carry
            rs = pl.ds(pl.multiple_of(t * batch, batch), batch)
            nxr = ar * xr - ai * xi + bu_ref[prev, rs, re]
            nxi = ar * xi + ai * xr + bu_ref[prev, rs, im]
            xs_ref[prev, rs, re] = nxr.astype(BF16)
            xs_ref[prev, rs, im] = nxi.astype(BF16)
            return nxr, nxi

        xr, xi = lax.fori_loop(0, TT, step, (st_ref[:, re], st_ref[:, im]), unroll=4)
        st_ref[:, re] = xr
        st_ref[:, im] = xi


def _s5(u, bre, bim, are, aim, cre, cim, d, wglu, bglu, gs):
    batch, L, _ = u.shape
    rows = TT * batch
    tiles = L // TT
    half = SSM_REAL // 2
    c2 = lambda s: (0, 0)
    c3 = lambda s: (0, 0, 0)
    src = (jnp.arange(rows) % batch) * TT + jnp.arange(rows) // batch
    perm = (src[:, None] == jnp.arange(rows)[None, :]).astype(BF16)
    return pl.pallas_call(
        functools.partial(_s5_kernel, batch=batch),
        grid=(tiles + 2,),
        in_specs=[
            pl.BlockSpec((batch, TT, SSM_WIDTH), lambda s: (0, jnp.minimum(s, tiles - 1), 0)),
            pl.BlockSpec((rows, rows), c2),
            pl.BlockSpec((rows, rows), c2),
            pl.BlockSpec((2, MXU_DIM, half), c3),
            pl.BlockSpec((2, MXU_DIM, half), c3),
            pl.BlockSpec((1, SSM_REAL), c2),
            pl.BlockSpec((1, SSM_REAL), c2),
            pl.BlockSpec((2, half, MXU_DIM), c3),
            pl.BlockSpec((2, half, MXU_DIM), c3),
            pl.BlockSpec((1, SSM_WIDTH), c2),
            pl.BlockSpec((SSM_WIDTH, SSM_WIDTH), c2),
            pl.BlockSpec((1, SSM_WIDTH), c2),
            pl.BlockSpec((1, SSM_WIDTH), c2),
        ],
        out_specs=pl.BlockSpec((batch, TT, SSM_WIDTH), lambda s: (0, jnp.maximum(s - 2, 0), 0)),
        out_shape=jax.ShapeDtypeStruct((batch, L, SSM_WIDTH), BF16),
        scratch_shapes=[
            pltpu.VMEM((3, rows, SSM_WIDTH), BF16),
            pltpu.VMEM((2, rows, 2 * SSM_REAL), F32),
            pltpu.VMEM((2, rows, 2 * SSM_REAL), BF16),
            pltpu.VMEM((batch, 2 * SSM_REAL), F32),
        ],
        compiler_params=pltpu.CompilerParams(
            dimension_semantics=("arbitrary",),
            vmem_limit_bytes=VMEM_LIMIT_BYTES),
        name="s5_glu",
    )(u, perm, perm.T, bre, bim, are, aim, cre, cim, d, wglu, bglu, gs)


def _attention_stages(qt_ref, k_ref, vt_ref, mlo_ref, z0_ref, sp0_ref, z_ref, sp_ref, acc_ref,
                      r_ref, rall_ref, rmin_ref, write_tile, add_tile):
    srow = lax.broadcasted_iota(jnp.int32, (TK, TK), 0)
    scol = lax.broadcasted_iota(jnp.int32, (TK, TK), 1)
    tri = jnp.where(scol >= srow, 1.0, 0.0).astype(BF16)
    drow = lax.broadcasted_iota(jnp.int32, (LANES, TQ), 0)

    def init():
        row = lax.broadcasted_iota(jnp.int32, (TK, TQ), 0)
        col = lax.broadcasted_iota(jnp.int32, (TK, TQ), 1)
        mlo_ref[...] = jnp.where(row < col, 1.0, 0.0)

    def weights(z, s, mask):
        return (jnp.exp(jnp.minimum(z - s, 0.0)) * mask).astype(BF16)

    def k_block(kb):
        return k_ref[0, pl.ds(pl.multiple_of(kb * TK, TK), TK), :]

    def softplus(z):
        return jnp.maximum(z, 0.0) + jnp.log(1.0 + jnp.exp(-jnp.abs(z)))

    def cumsum(sp):
        return jnp.dot(tri, sp, preferred_element_type=F32)

    def head_queries(qi):
        qt = qt_ref[0, 0, qi]
        zero = jnp.zeros_like(qt)
        return (jnp.where(drow < HEAD_DIM, qt, zero), jnp.where(drow >= HEAD_DIM, qt, zero))

    def far_kb(qi, j):
        return jnp.maximum(2 * qi - 1 - j, 0)

    def stage_a(qi, slot):
        qh = head_queries(qi)
        k_top, k_lo = k_block(2 * qi + 1), k_block(2 * qi)
        k_far = [k_block(far_kb(qi, j)) for j in range(N_FAR)]
        for h in range(2):
            z0 = jnp.dot(k_top, qh[h][:, TK:], preferred_element_type=F32)
            z0_ref[slot, h] = z0
            sp0_ref[slot, h] = (softplus(z0) * mlo_ref[:, :TK]).astype(BF16)
            z1 = jnp.dot(k_lo, qh[h], preferred_element_type=F32)
            z_ref[slot, h, 0] = z1
            sp_ref[slot, h, 0] = (softplus(z1) * mlo_ref[...]).astype(BF16)
            for j in range(N_FAR):
                z = jnp.dot(k_far[j], qh[h], preferred_element_type=F32)
                z_ref[slot, h, 1 + j] = z
                sp_ref[slot, h, 1 + j] = softplus(z).astype(BF16)

    def stage_b1(slot):
        return [(cumsum(sp0_ref[slot, h]), [cumsum(sp_ref[slot, h, j]) for j in range(1 + N_FAR)])
                for h in range(2)]

    def stage_b2(qi, slot, sums):
        kbs = [2 * qi + 1, 2 * qi] + [far_kb(qi, j) for j in range(N_FAR)]
        vs = [vt_ref[0, 0, kb] for kb in kbs]
        no_far = jnp.where(qi == 0, NO_KEYS, 0.0).astype(F32)
        rmin, accs = None, []
        for h in range(2):
            c0, cs = sums[h]
            a0 = weights(z0_ref[slot, h], c0, mlo_ref[:, :TK])
            r = jnp.concatenate([jnp.zeros((1, TK), F32), c0[0:1, :]], axis=1)
            a1 = weights(z_ref[slot, h, 0], cs[0] + r, mlo_ref[...])
            r = r + cs[0][0:1, :] + no_far
            parts = [jnp.concatenate([jnp.zeros((TK, TK), BF16), a0], axis=1), a1]
            for j in range(N_FAR):
                c = cs[1 + j]
                parts.append(jnp.exp(z_ref[slot, h, 1 + j] - (c + r)).astype(BF16))
                r = r + c[0:1, :]
            hs = slice(h * HEAD_DIM, (h + 1) * HEAD_DIM)
            v_cat = jnp.concatenate([v[hs, :] for v in vs], axis=1)
            accs.append(jnp.dot(v_cat, jnp.concatenate(parts, axis=0),
                                preferred_element_type=F32))
            rall_ref[qi, h] = r
            m = jnp.min(r)
            rmin = m if rmin is None else jnp.minimum(rmin, m)
        rmin_ref[qi] = rmin
        write_tile(qi, jnp.concatenate(accs, axis=0).T)

    def far_block(kb, qh):
        kblk = k_block(kb)
        vblk = vt_ref[0, 0, kb]
        rmin = None
        zs = [jnp.dot(kblk, qh[h], preferred_element_type=F32) for h in range(2)]
        cs = [cumsum(softplus(z).astype(BF16)) for z in zs]
        for h in range(2):
            z, c = zs[h], cs[h]
            r_old = r_ref[h]
            a = jnp.exp(z - (c + r_old)).astype(BF16)
            acc_ref[h] += jnp.dot(vblk[h * HEAD_DIM:(h + 1) * HEAD_DIM, :], a,
                                  preferred_element_type=F32)
            r_new = r_old + c[0:1, :]
            r_ref[h] = r_new
            m = jnp.min(r_new)
            rmin = m if rmin is None else jnp.minimum(rmin, m)
        return rmin

    def finish(nq):
        def one_tile(qi, carry):
            kb0 = 2 * qi - 1 - N_FAR
            rmin0 = rmin_ref[qi]

            @pl.when(jnp.logical_and(kb0 >= 0, rmin0 < SKIP_THRESHOLD))
            def _():
                qh = head_queries(qi)
                acc_ref[...] = jnp.zeros_like(acc_ref)
                for h in range(2):
                    r_ref[h] = rall_ref[qi, h]

                def cond(c):
                    kb, rmin = c
                    return jnp.logical_and(kb >= 0, rmin < SKIP_THRESHOLD)

                def body(c):
                    kb, _ = c
                    return kb - 1, far_block(kb, qh)

                lax.while_loop(cond, body, (kb0, rmin0))
                acc = jnp.concatenate([acc_ref[0], acc_ref[1]], axis=0)
                add_tile(qi, acc.T)

            return carry

        lax.fori_loop(0, nq, one_tile, 0)

    return init, stage_a, stage_b1, stage_b2, finish


def _attn_mlp_kernel(qt_ref, k_ref, vt_ref, x_ref, ssm_ref, ga_ref, wout_ref, g2_ref, w1_ref, w2_ref,
                     o_ref, sb_ref, hn_ref, mlo_ref, z0_ref, sp0_ref, z_ref, sp_ref, acc_ref, r_ref,
                     rall_ref, rmin_ref):
    s, t = pl.program_id(0), pl.program_id(1)
    last = pl.num_programs(0) - 1
    nq = qt_ref.shape[2]
    wslot = lax.rem(s, 2)
    rslot = 1 - wslot

    def tile_rows(qi):
        return pl.ds(pl.multiple_of(qi * TQ, TQ), TQ)

    def write_tile(qi, tile):
        sb_ref[wslot, t, tile_rows(qi), :] = tile

    def add_tile(qi, tile):
        sb_ref[wslot, t, tile_rows(qi), :] += tile

    init, stage_a, stage_b1, stage_b2, finish = _attention_stages(
        qt_ref, k_ref, vt_ref, mlo_ref, z0_ref, sp0_ref, z_ref, sp_ref, acc_ref, r_ref, rall_ref,
        rmin_ref, write_tile, add_tile)

    def mlp_head():
        rows = pl.ds(pl.multiple_of(t * TM_MLP, TM_MLP), TM_MLP)
        sb = jnp.concatenate([sb_ref[rslot, p, rows, :] for p in range(HEAD_PAIRS)], axis=1)
        an = (_rms(sb) * ga_ref[...]).astype(BF16)
        h = x_ref[0] + jnp.dot(an, wout_ref[:SB_WIDTH, :], preferred_element_type=F32)
        h = h + jnp.dot(ssm_ref[0], wout_ref[SB_WIDTH:, :], preferred_element_type=F32)
        hn_ref[...] = (_rms(h) * g2_ref[...]).astype(BF16)
        o_ref[0] = h

    def mlp_chunk(c):
        cols = slice(c * FF_SUB, (c + 1) * FF_SUB)
        a = jnp.dot(hn_ref[...], w1_ref[:, cols], preferred_element_type=F32)
        a = jnp.square(jnp.maximum(a, 0.0)).astype(BF16)
        o_ref[0] += jnp.dot(a, w2_ref[cols, :], preferred_element_type=F32)

    def step(attn, mlp):
        if attn:
            init()
            stage_a(0, 0)
        if mlp:
            mlp_head()

        def tile(qi, slot, next_slot, has_next=True):
            if attn:
                sums = stage_b1(slot)
                if has_next:
                    stage_a(qi + 1, next_slot)
            if mlp:
                mlp_chunk(qi)
            if attn:
                stage_b2(qi, slot, sums)

        for qi in range(nq):
            tile(qi, qi % 2, 1 - qi % 2, has_next=qi + 1 < nq)
        if attn:
            finish(nq)

    @pl.when(s == 0)
    def _():
        step(True, False)

    @pl.when(jnp.logical_and(s > 0, s < last))
    def _():
        step(True, True)

    @pl.when(s == last)
    def _():
        step(False, True)


def _attn_mlp(qt, k, vt, x, ssm, ga, wout, g2, w1, w2):
    B, _, nq, _, _ = qt.shape
    L, D = x.shape[1], x.shape[2]
    nk = vt.shape[2]
    assert nq % 2 == 0 and nq >= 2 and nq == D_FF // FF_SUB and L // TM_MLP == HEAD_PAIRS
    const = lambda s, t: (0, 0)
    cur = lambda s: jnp.minimum(s, B - 1)
    prv = lambda s: jnp.maximum(s - 1, 0)
    tok = lambda s, t: jnp.where(s == 0, 0, t)
    return pl.pallas_call(
        _attn_mlp_kernel,
        grid=(B + 1, HEAD_PAIRS),
        in_specs=[
            pl.BlockSpec((1, 1, nq, LANES, TQ), lambda s, t: (cur(s), t, 0, 0, 0)),
            pl.BlockSpec((1, L, LANES), lambda s, t: (cur(s), 0, t)),
            pl.BlockSpec((1, 1, nk, LANES, TK), lambda s, t: (cur(s), t, 0, 0, 0)),
            pl.BlockSpec((1, TM_MLP, D), lambda s, t: (prv(s), tok(s, t), 0)),
            pl.BlockSpec((1, TM_MLP, SSM_WIDTH), lambda s, t: (prv(s), tok(s, t), 0)),
            pl.BlockSpec((1, SB_WIDTH), const),
            pl.BlockSpec((D, D), const),
            pl.BlockSpec((1, D), const),
            pl.BlockSpec((D, D_FF), const),
            pl.BlockSpec((D_FF, D), const),
        ],
        out_specs=pl.BlockSpec((1, TM_MLP, D), lambda s, t: (prv(s), tok(s, t), 0)),
        out_shape=jax.ShapeDtypeStruct((B, L, D), F32),
        scratch_shapes=[
            pltpu.VMEM((2, HEAD_PAIRS, L, LANES), F32),
            pltpu.VMEM((TM_MLP, D), BF16),
            pltpu.VMEM((TK, TQ), F32),
            pltpu.VMEM((2, 2, TK, TK), F32),
            pltpu.VMEM((2, 2, TK, TK), BF16),
            pltpu.VMEM((2, 2, 1 + N_FAR, TK, TQ), F32),
            pltpu.VMEM((2, 2, 1 + N_FAR, TK, TQ), BF16),
            pltpu.VMEM((2, HEAD_DIM, TQ), F32),
            pltpu.VMEM((2, 1, TQ), F32),
            pltpu.VMEM((nq, 2, 1, TQ), F32),
            pltpu.SMEM((nq,), F32),
        ],
        compiler_params=pltpu.CompilerParams(
            dimension_semantics=("arbitrary", "arbitrary"),
            vmem_limit_bytes=VMEM_LIMIT_BYTES),
        name="attn_mlp",
    )(qt, k, vt, x, ssm, ga, wout, g2, w1, w2)


def _s5_params(lam_re, lam_im, log_dt, b_re, b_im, c_re, c_im):
    G, P, H = SSM_GROUPS, SSM_STATE, SSM_GROUP
    lr, li = lam_re.astype(F32), lam_im.astype(F32)
    dt = jnp.exp(log_dt.astype(F32))[:, None]
    mag = jnp.exp(lr * dt)
    are, aim = mag * jnp.cos(li * dt), mag * jnp.sin(li * dt)
    den = lr * lr + li * li
    wr = ((are - 1.0) * lr + aim * li) / den
    wi = (aim * lr - (are - 1.0) * li) / den
    bbr = wr[:, :, None] * b_re.astype(F32) - wi[:, :, None] * b_im.astype(F32)
    bbi = wr[:, :, None] * b_im.astype(F32) + wi[:, :, None] * b_re.astype(F32)
    are, aim = are.reshape(1, G * P), aim.reshape(1, G * P)
    gpb = MXU_DIM // H
    eye = jnp.eye(gpb, dtype=F32)

    def b_layout(b):
        b = b.reshape(G // gpb, gpb, P, H)
        return jnp.einsum("kgph,gf->kghfp", b, eye).reshape(G // gpb, gpb * H, gpb * P).astype(BF16)

    def c_layout(c):
        c = c.reshape(G // gpb, gpb, H, P)
        return jnp.einsum("kghp,gf->kgpfh", c, eye).reshape(G // gpb, gpb * P, gpb * H).astype(BF16)

    return (b_layout(bbr), b_layout(bbi), are, aim,
            c_layout(c_re.astype(F32)), c_layout(-c_im.astype(F32)))


def kernel(x, norm1_g, w_in, q_norm_g, k_norm_g, ssm_lambda_re, ssm_lambda_im, ssm_log_dt,
           ssm_b_re, ssm_b_im, ssm_c_re, ssm_c_im, ssm_d, w_glu, b_glu, attn_out_g,
           ssm_out_g, w_out, norm2_g, w_mlp_in, w_mlp_out):
    B, L, D = x.shape
    assert (D, L % TM_PROJ, L % TM_MLP, L % TT) == (D_MODEL, 0, 0, 0)
    heads = SB_WIDTH // HEAD_DIM
    row = lambda g: g.astype(F32).reshape(1, -1)

    head_id = jnp.arange(SB_WIDTH) // HEAD_DIM
    hsel = (head_id[:, None] == head_id[None, :]).astype(BF16) * (1.0 / HEAD_DIM)
    gq = row(jnp.tile(q_norm_g.astype(F32), heads)) * (1.0 / math.sqrt(HEAD_DIM))
    gk = row(jnp.tile(k_norm_g.astype(F32), heads))

    qt, k, vt, u = _inproj(x.astype(F32), row(norm1_g), w_in.astype(BF16), hsel, gq, gk)

    bre, bim, are, aim, cre, cim = _s5_params(
        ssm_lambda_re, ssm_lambda_im, ssm_log_dt, ssm_b_re, ssm_b_im, ssm_c_re, ssm_c_im)
    ssm = _s5(u, bre, bim, are, aim, cre, cim,
              row(ssm_d), w_glu.astype(BF16), row(b_glu), row(ssm_out_g))

    out = _attn_mlp(qt, k, vt, x.astype(F32), ssm, row(attn_out_g), w_out.astype(BF16),
                    row(norm2_g), w_mlp_in.astype(BF16), w_mlp_out.astype(BF16))
    return out.astype(x.dtype)
```

```python
import functools
import math

import jax
import jax.numpy as jnp
from jax import lax
from jax.experimental import pallas as pl
from jax.experimental.pallas import tpu as pltpu

F32 = jnp.float32
BF16 = jnp.bfloat16

D_MODEL = 1024
SB_WIDTH = 512
HEAD_DIM = 64
HEAD_PAIRS = SB_WIDTH // (2 * HEAD_DIM)
SSM_WIDTH = 512
SSM_GROUP = 16
SSM_GROUPS = 32
SSM_STATE = 64
SSM_REAL = SSM_GROUPS * SSM_STATE
D_FF = 4 * D_MODEL
EPS = 1e-6

LANES = 128
MXU_DIM = 256
VMEM_LIMIT_BYTES = 56 * 1024 * 1024

TM_PROJ = 1024
TQ = 256
TK = 128
TT = 32
SCAN_W = 512
TM_MLP = 512
FF_SUB = 512

SKIP_THRESHOLD = 88.0
N_FAR = 1
NO_KEYS = 1e30


def _rms(x):
    return x * lax.rsqrt(jnp.mean(x * x, axis=-1, keepdims=True) + EPS)


def _inproj_kernel(x_ref, g1_ref, win_ref, hsel_ref, gq_ref, gk_ref,
                   qt_ref, k_ref, vt_ref, u_ref):
    x = x_ref[0]
    xn = (_rms(x) * g1_ref[...]).astype(BF16)
    proj = jnp.dot(xn, win_ref[...], preferred_element_type=F32)
    q = proj[:, 0 * SB_WIDTH:1 * SB_WIDTH]
    k = proj[:, 1 * SB_WIDTH:2 * SB_WIDTH]
    v = proj[:, 2 * SB_WIDTH:3 * SB_WIDTH]
    u = proj[:, 3 * SB_WIDTH:]
    hsel = hsel_ref[...]
    msq = jnp.dot((q * q).astype(BF16), hsel, preferred_element_type=F32)
    msk = jnp.dot((k * k).astype(BF16), hsel, preferred_element_type=F32)
    qn = q * lax.rsqrt(msq + EPS) * gq_ref[...]
    kn = k * lax.rsqrt(msk + EPS) * gk_ref[...]
    k_ref[0] = kn.astype(BF16)
    u_ref[0] = u.astype(BF16)
    for j in range(TM_PROJ // TQ):
        qt = qn[j * TQ:(j + 1) * TQ, :].T
        qt_ref[0, :, j] = qt.astype(BF16).reshape(HEAD_PAIRS, LANES, TQ)
    for j in range(TM_PROJ // TK):
        vt = v[j * TK:(j + 1) * TK, :].T
        vt_ref[0, :, j] = vt.astype(BF16).reshape(HEAD_PAIRS, LANES, TK)


def _inproj(x, g1, win, hsel, gq, gk):
    B, L, D = x.shape
    nq, nk = L // TQ, L // TK
    const = lambda b, t: (0, 0)
    return pl.pallas_call(
        _inproj_kernel,
        grid=(B, L // TM_PROJ),
        in_specs=[
            pl.BlockSpec((1, TM_PROJ, D), lambda b, t: (b, t, 0)),
            pl.BlockSpec((1, D), const),
            pl.BlockSpec((D, 4 * SB_WIDTH), const),
            pl.BlockSpec((SB_WIDTH, SB_WIDTH), const),
            pl.BlockSpec((1, SB_WIDTH), const),
            pl.BlockSpec((1, SB_WIDTH), const),
        ],
        out_specs=[
            pl.BlockSpec((1, HEAD_PAIRS, TM_PROJ // TQ, LANES, TQ), lambda b, t: (b, 0, t, 0, 0)),
            pl.BlockSpec((1, TM_PROJ, SB_WIDTH), lambda b, t: (b, t, 0)),
            pl.BlockSpec((1, HEAD_PAIRS, TM_PROJ // TK, LANES, TK), lambda b, t: (b, 0, t, 0, 0)),
            pl.BlockSpec((1, TM_PROJ, SSM_WIDTH), lambda b, t: (b, t, 0)),
        ],
        out_shape=[
            jax.ShapeDtypeStruct((B, HEAD_PAIRS, nq, LANES, TQ), BF16),
            jax.ShapeDtypeStruct((B, L, SB_WIDTH), BF16),
            jax.ShapeDtypeStruct((B, HEAD_PAIRS, nk, LANES, TK), BF16),
            jax.ShapeDtypeStruct((B, L, SSM_WIDTH), BF16),
        ],
        compiler_params=pltpu.CompilerParams(
            dimension_semantics=("parallel", "parallel"),
            vmem_limit_bytes=VMEM_LIMIT_BYTES),
        name="inproj",
    )(x, g1, win, hsel, gq, gk)


def _gelu_tanh(x):
    c = math.sqrt(2.0 / math.pi)
    return 0.5 * x * (1.0 + jnp.tanh(c * (x + 0.044715 * (x * x * x))))


def _s5_kernel(u_ref, perm_ref, permt_ref, bre_ref, bim_ref, are_ref, aim_ref, cre_ref, cim_ref,
               d_ref, wglu_ref, bglu_ref, gs_ref, o_ref, ub_ref, bu_ref, xs_ref, st_ref, *, batch):
    s = pl.program_id(0)

    @pl.when(s == 0)
    def _():
        for ref in (ub_ref, bu_ref, xs_ref, st_ref):
            ref[...] = jnp.zeros_like(ref)

    rows = batch * TT
    half = SSM_REAL // 2
    cur = lax.rem(s, 2)
    prev = 1 - cur

    def stage1_b_bar_u(u, kb):
        ub = u[:, kb * MXU_DIM:(kb + 1) * MXU_DIM]
        bu_ref[cur, :, kb * half:(kb + 1) * half] = jnp.dot(
            ub, bre_ref[kb], preferred_element_type=F32)
        bu_ref[cur, :, SSM_REAL + kb * half:SSM_REAL + (kb + 1) * half] = jnp.dot(
            ub, bim_ref[kb], preferred_element_type=F32)

    ys = []
    for ob in range(2):
        y = jnp.dot(xs_ref[cur, :, ob * half:(ob + 1) * half], cre_ref[ob],
                    preferred_element_type=F32)
        y = y + jnp.dot(xs_ref[cur, :, SSM_REAL + ob * half:SSM_REAL + (ob + 1) * half],
                        cim_ref[ob], preferred_element_type=F32)
        ys.append(y)
    y = jnp.concatenate(ys, axis=1) + d_ref[...] * ub_ref[lax.rem(s + 1, 3)].astype(F32)

    u = jnp.dot(perm_ref[...], u_ref[...].reshape(rows, SSM_WIDTH),
                preferred_element_type=F32).astype(BF16)
    ub_ref[lax.rem(s, 3)] = u
    stage1_b_bar_u(u, 0)

    y = _gelu_tanh(y)
    gate = jax.nn.sigmoid(
        jnp.dot(y.astype(BF16), wglu_ref[...], preferred_element_type=F32) + bglu_ref[...])
    stage1_b_bar_u(u, 1)

    out_tb = (_rms(y * gate) * gs_ref[...]).astype(BF16)
    out_bt = jnp.dot(permt_ref[...], out_tb, preferred_element_type=F32)
    o_ref[...] = out_bt.astype(o_ref.dtype).reshape(batch, TT, SSM_WIDTH)

    for sc in range(SSM_REAL // SCAN_W):
        re = slice(sc * SCAN_W, (sc + 1) * SCAN_W)
        im = slice(SSM_REAL + sc * SCAN_W, SSM_REAL + (sc + 1) * SCAN_W)
        ar = jnp.broadcast_to(are_ref[:, re], (batch, SCAN_W))
        ai = jnp.broadcast_to(aim_ref[:, re], (batch, SCAN_W))

        def step(t, carry, re=re, im=im, ar=ar, ai=ai):
            xr, xi = carry
            rs = pl.ds(pl.multiple_of(t * batch, batch), batch)
            nxr = ar * xr - ai * xi + bu_ref[prev, rs, re]
            nxi = ar * xi + ai * xr + bu_ref[prev, rs, im]
            xs_ref[prev, rs, re] = nxr.astype(BF16)
            xs_ref[prev, rs, im] = nxi.astype(BF16)
            return nxr, nxi

        xr, xi = lax.fori_loop(0, TT, step, (st_ref[:, re], st_ref[:, im]), unroll=4)
        st_ref[:, re] = xr
        st_ref[:, im] = xi


def _s5(u, bre, bim, are, aim, cre, cim, d, wglu, bglu, gs):
    batch, L, _ = u.shape
    rows = TT * batch
    tiles = L // TT
    half = SSM_REAL // 2
    c2 = lambda s: (0, 0)
    c3 = lambda s: (0, 0, 0)
    src = (jnp.arange(rows) % batch) * TT + jnp.arange(rows) // batch
    perm = (src[:, None] == jnp.arange(rows)[None, :]).astype(BF16)
    return pl.pallas_call(
        functools.partial(_s5_kernel, batch=batch),
        grid=(tiles + 2,),
        in_specs=[
            pl.BlockSpec((batch, TT, SSM_WIDTH), lambda s: (0, jnp.minimum(s, tiles - 1), 0)),
            pl.BlockSpec((rows, rows), c2),
            pl.BlockSpec((rows, rows), c2),
            pl.BlockSpec((2, MXU_DIM, half), c3),
            pl.BlockSpec((2, MXU_DIM, half), c3),
            pl.BlockSpec((1, SSM_REAL), c2),
            pl.BlockSpec((1, SSM_REAL), c2),
            pl.BlockSpec((2, half, MXU_DIM), c3),
            pl.BlockSpec((2, half, MXU_DIM), c3),
            pl.BlockSpec((1, SSM_WIDTH), c2),
            pl.BlockSpec((SSM_WIDTH, SSM_WIDTH), c2),
            pl.BlockSpec((1, SSM_WIDTH), c2),
            pl.BlockSpec((1, SSM_WIDTH), c2),
        ],
        out_specs=pl.BlockSpec((batch, TT, SSM_WIDTH), lambda s: (0, jnp.maximum(s - 2, 0), 0)),
        out_shape=jax.ShapeDtypeStruct((batch, L, SSM_WIDTH), BF16),
        scratch_shapes=[
            pltpu.VMEM((3, rows, SSM_WIDTH), BF16),
            pltpu.VMEM((2, rows, 2 * SSM_REAL), F32),
            pltpu.VMEM((2, rows, 2 * SSM_REAL), BF16),
            pltpu.VMEM((batch, 2 * SSM_REAL), F32),
        ],
        compiler_params=pltpu.CompilerParams(
            dimension_semantics=("arbitrary",),
            vmem_limit_bytes=VMEM_LIMIT_BYTES),
        name="s5_glu",
    )(u, perm, perm.T, bre, bim, are, aim, cre, cim, d, wglu, bglu, gs)


def _attention_stages(qt_ref, k_ref, vt_ref, mlo_ref, z0_ref, sp0_ref, z_ref, sp_ref, acc_ref,
                      r_ref, rall_ref, rmin_ref, write_tile, add_tile):
    srow = lax.broadcasted_iota(jnp.int32, (TK, TK), 0)
    scol = lax.broadcasted_iota(jnp.int32, (TK, TK), 1)
    tri = jnp.where(scol >= srow, 1.0, 0.0).astype(BF16)
    drow = lax.broadcasted_iota(jnp.int32, (LANES, TQ), 0)

    def init():
        row = lax.broadcasted_iota(jnp.int32, (TK, TQ), 0)
        col = lax.broadcasted_iota(jnp.int32, (TK, TQ), 1)
        mlo_ref[...] = jnp.where(row < col, 1.0, 0.0)

    def weights(z, s, mask):
        return (jnp.exp(jnp.minimum(z - s, 0.0)) * mask).astype(BF16)

    def k_block(kb):
        return k_ref[0, pl.ds(pl.multiple_of(kb * TK, TK), TK), :]

    def softplus(z):
        return jnp.maximum(z, 0.0) + jnp.log(1.0 + jnp.exp(-jnp.abs(z)))

    def cumsum(sp):
        return jnp.dot(tri, sp, preferred_element_type=F32)

    def head_queries(qi):
        qt = qt_ref[0, 0, qi]
        zero = jnp.zeros_like(qt)
        return (jnp.where(drow < HEAD_DIM, qt, zero), jnp.where(drow >= HEAD_DIM, qt, zero))

    def far_kb(qi, j):
        return jnp.maximum(2 * qi - 1 - j, 0)

    def stage_a(qi, slot):
        qh = head_queries(qi)
        k_top, k_lo = k_block(2 * qi + 1), k_block(2 * qi)
        k_far = [k_block(far_kb(qi, j)) for j in range(N_FAR)]
        for h in range(2):
            z0 = jnp.dot(k_top, qh[h][:, TK:], preferred_element_type=F32)
            z0_ref[slot, h] = z0
            sp0_ref[slot, h] = (softplus(z0) * mlo_ref[:, :TK]).astype(BF16)
            z1 = jnp.dot(k_lo, qh[h], preferred_element_type=F32)
            z_ref[slot, h, 0] = z1
            sp_ref[slot, h, 0] = (softplus(z1) * mlo_ref[...]).astype(BF16)
            for j in range(N_FAR):
                z = jnp.dot(k_far[j], qh[h], preferred_element_type=F32)
                z_ref[slot, h, 1 + j] = z
                sp_ref[slot, h, 1 + j] = softplus(z).astype(BF16)

    def stage_b1(slot):
        return [(cumsum(sp0_ref[slot, h]), [cumsum(sp_ref[slot, h, j]) for j in range(1 + N_FAR)])
                for h in range(2)]

    def stage_b2(qi, slot, sums):
        kbs = [2 * qi + 1, 2 * qi] + [far_kb(qi, j) for j in range(N_FAR)]
        vs = [vt_ref[0, 0, kb] for kb in kbs]
        no_far = jnp.where(qi == 0, NO_KEYS, 0.0).astype(F32)
        rmin, accs = None, []
        for h in range(2):
            c0, cs = sums[h]
            a0 = weights(z0_ref[slot, h], c0, mlo_ref[:, :TK])
            r = jnp.concatenate([jnp.zeros((1, TK), F32), c0[0:1, :]], axis=1)
            a1 = weights(z_ref[slot, h, 0], cs[0] + r, mlo_ref[...])
            r = r + cs[0][0:1, :] + no_far
            parts = [jnp.concatenate([jnp.zeros((TK, TK), BF16), a0], axis=1), a1]
            for j in range(N_FAR):
                c = cs[1 + j]
                parts.append(jnp.exp(z_ref[slot, h, 1 + j] - (c + r)).astype(BF16))
                r = r + c[0:1, :]
            hs = slice(h * HEAD_DIM, (h + 1) * HEAD_DIM)
            v_cat = jnp.concatenate([v[hs, :] for v in vs], axis=1)
            accs.append(jnp.dot(v_cat, jnp.concatenate(parts, axis=0),
                                preferred_element_type=F32))
            rall_ref[qi, h] = r
            m = jnp.min(r)
            rmin = m if rmin is None else jnp.minimum(rmin, m)
        rmin_ref[qi] = rmin
        write_tile(qi, jnp.concatenate(accs, axis=0).T)

    def far_block(kb, qh):
        kblk = k_block(kb)
        vblk = vt_ref[0, 0, kb]
        rmin = None
        zs = [jnp.dot(kblk, qh[h], preferred_element_type=F32) for h in range(2)]
        cs = [cumsum(softplus(z).astype(BF16)) for z in zs]
        for h in range(2):
            z, c = zs[h], cs[h]
            r_old = r_ref[h]
            a = jnp.exp(z - (c + r_old)).astype(BF16)
            acc_ref[h] += jnp.dot(vblk[h * HEAD_DIM:(h + 1) * HEAD_DIM, :], a,
                                  preferred_element_type=F32)
            r_new = r_old + c[0:1, :]
            r_ref[h] = r_new
            m = jnp.min(r_new)
            rmin = m if rmin is None else jnp.minimum(rmin, m)
        return rmin

    def finish(nq):
        def one_tile(qi, carry):
            kb0 = 2 * qi - 1 - N_FAR
            rmin0 = rmin_ref[qi]

            @pl.when(jnp.logical_and(kb0 >= 0, rmin0 < SKIP_THRESHOLD))
            def _():
                qh = head_queries(qi)
                acc_ref[...] = jnp.zeros_like(acc_ref)
                for h in range(2):
                    r_ref[h] = rall_ref[qi, h]

                def cond(c):
                    kb, rmin = c
                    return jnp.logical_and(kb >= 0, rmin < SKIP_THRESHOLD)

                def body(c):
                    kb, _ = c
                    return kb - 1, far_block(kb, qh)

                lax.while_loop(cond, body, (kb0, rmin0))
                acc = jnp.concatenate([acc_ref[0], acc_ref[1]], axis=0)
                add_tile(qi, acc.T)

            return carry

        lax.fori_loop(0, nq, one_tile, 0)

    return init, stage_a, stage_b1, stage_b2, finish


def _attn_mlp_kernel(qt_ref, k_ref, vt_ref, x_ref, ssm_ref, ga_ref, wout_ref, g2_ref, w1_ref, w2_ref,
                     o_ref, sb_ref, hn_ref, mlo_ref, z0_ref, sp0_ref, z_ref, sp_ref, acc_ref, r_ref,
                     rall_ref, rmin_ref):
    s, t = pl.program_id(0), pl.program_id(1)
    last = pl.num_programs(0) - 1
    nq = qt_ref.shape[2]
    wslot = lax.rem(s, 2)
    rslot = 1 - wslot

    def tile_rows(qi):
        return pl.ds(pl.multiple_of(qi * TQ, TQ), TQ)

    def write_tile(qi, tile):
        sb_ref[wslot, t, tile_rows(qi), :] = tile

    def add_tile(qi, tile):
        sb_ref[wslot, t, tile_rows(qi), :] += tile

    init, stage_a, stage_b1, stage_b2, finish = _attention_stages(
        qt_ref, k_ref, vt_ref, mlo_ref, z0_ref, sp0_ref, z_ref, sp_ref, acc_ref, r_ref, rall_ref,
        rmin_ref, write_tile, add_tile)

    def mlp_head():
        rows = pl.ds(pl.multiple_of(t * TM_MLP, TM_MLP), TM_MLP)
        sb = jnp.concatenate([sb_ref[rslot, p, rows, :] for p in range(HEAD_PAIRS)], axis=1)
        an = (_rms(sb) * ga_ref[...]).astype(BF16)
        h = x_ref[0] + jnp.dot(an, wout_ref[:SB_WIDTH, :], preferred_element_type=F32)
        h = h + jnp.dot(ssm_ref[0], wout_ref[SB_WIDTH:, :], preferred_element_type=F32)
        hn_ref[...] = (_rms(h) * g2_ref[...]).astype(BF16)
        o_ref[0] = h

    def mlp_chunk(c):
        cols = slice(c * FF_SUB, (c + 1) * FF_SUB)
        a = jnp.dot(hn_ref[...], w1_ref[:, cols], preferred_element_type=F32)
        a = jnp.square(jnp.maximum(a, 0.0)).astype(BF16)
        o_ref[0] += jnp.dot(a, w2_ref[cols, :], preferred_element_type=F32)

    def step(attn, mlp):
        if attn:
            init()
            stage_a(0, 0)
        if mlp:
            mlp_head()

        def tile(qi, slot, next_slot, has_next=True):
            if attn:
                sums = stage_b1(slot)
                if has_next:
                    stage_a(qi + 1, next_slot)
            if mlp:
                mlp_chunk(qi)
            if attn:
                stage_b2(qi, slot, sums)

        for qi in range(nq):
            tile(qi, qi % 2, 1 - qi % 2, has_next=qi + 1 < nq)
        if attn:
            finish(nq)

    @pl.when(s == 0)
    def _():
        step(True, False)

    @pl.when(jnp.logical_and(s > 0, s < last))
    def _():
        step(True, True)

    @pl.when(s == last)
    def _():
        step(False, True)


def _attn_mlp(qt, k, vt, x, ssm, ga, wout, g2, w1, w2):
    B, _, nq, _, _ = qt.shape
    L, D = x.shape[1], x.shape[2]
    nk = vt.shape[2]
    assert nq % 2 == 0 and nq >= 2 and nq == D_FF // FF_SUB and L // TM_MLP == HEAD_PAIRS
    const = lambda s, t: (0, 0)
    cur = lambda s: jnp.minimum(s, B - 1)
    prv = lambda s: jnp.maximum(s - 1, 0)
    tok = lambda s, t: jnp.where(s == 0, 0, t)
    return pl.pallas_call(
        _attn_mlp_kernel,
        grid=(B + 1, HEAD_PAIRS),
        in_specs=[
            pl.BlockSpec((1, 1, nq, LANES, TQ), lambda s, t: (cur(s), t, 0, 0, 0)),
            pl.BlockSpec((1, L, LANES), lambda s, t: (cur(s), 0, t)),
            pl.BlockSpec((1, 1, nk, LANES, TK), lambda s, t: (cur(s), t, 0, 0, 0)),
            pl.BlockSpec((1, TM_MLP, D), lambda s, t: (prv(s), tok(s, t), 0)),
            pl.BlockSpec((1, TM_MLP, SSM_WIDTH), lambda s, t: (prv(s), tok(s, t), 0)),
            pl.BlockSpec((1, SB_WIDTH), const),
            pl.BlockSpec((D, D), const),
            pl.BlockSpec((1, D), const),
            pl.BlockSpec((D, D_FF), const),
            pl.BlockSpec((D_FF, D), const),
        ],
        out_specs=pl.BlockSpec((1, TM_MLP, D), lambda s, t: (prv(s), tok(s, t), 0)),
        out_shape=jax.ShapeDtypeStruct((B, L, D), F32),
        scratch_shapes=[
            pltpu.VMEM((2, HEAD_PAIRS, L, LANES), F32),
            pltpu.VMEM((TM_MLP, D), BF16),
            pltpu.VMEM((TK, TQ), F32),
            pltpu.VMEM((2, 2, TK, TK), F32),
            pltpu.VMEM((2, 2, TK, TK), BF16),
            pltpu.VMEM((2, 2, 1 + N_FAR, TK, TQ), F32),
            pltpu.VMEM((2, 2, 1 + N_FAR, TK, TQ), BF16),
            pltpu.VMEM((2, HEAD_DIM, TQ), F32),
            pltpu.VMEM((2, 1, TQ), F32),
            pltpu.VMEM((nq, 2, 1, TQ), F32),
            pltpu.SMEM((nq,), F32),
        ],
        compiler_params=pltpu.CompilerParams(
            dimension_semantics=("arbitrary", "arbitrary"),
            vmem_limit_bytes=VMEM_LIMIT_BYTES),
        name="attn_mlp",
    )(qt, k, vt, x, ssm, ga, wout, g2, w1, w2)


def _s5_params(lam_re, lam_im, log_dt, b_re, b_im, c_re, c_im):
    G, P, H = SSM_GROUPS, SSM_STATE, SSM_GROUP
    lr, li = lam_re.astype(F32), lam_im.astype(F32)
    dt = jnp.exp(log_dt.astype(F32))[:, None]
    mag = jnp.exp(lr * dt)
    are, aim = mag * jnp.cos(li * dt), mag * jnp.sin(li * dt)
    den = lr * lr + li * li
    wr = ((are - 1.0) * lr + aim * li) / den
    wi = (aim * lr - (are - 1.0) * li) / den
    bbr = wr[:, :, None] * b_re.astype(F32) - wi[:, :, None] * b_im.astype(F32)
    bbi = wr[:, :, None] * b_im.astype(F32) + wi[:, :, None] * b_re.astype(F32)
    are, aim = are.reshape(1, G * P), aim.reshape(1, G * P)
    gpb = MXU_DIM // H
    eye = jnp.eye(gpb, dtype=F32)

    def b_layout(b):
        b = b.reshape(G // gpb, gpb, P, H)
        return jnp.einsum("kgph,gf->kghfp", b, eye).reshape(G // gpb, gpb * H, gpb * P).astype(BF16)

    def c_layout(c):
        c = c.reshape(G // gpb, gpb, H, P)
        return jnp.einsum("kghp,gf->kgpfh", c, eye).reshape(G // gpb, gpb * P, gpb * H).astype(BF16)

    return (b_layout(bbr), b_layout(bbi), are, aim,
            c_layout(c_re.astype(F32)), c_layout(-c_im.astype(F32)))


def kernel(x, norm1_g, w_in, q_norm_g, k_norm_g, ssm_lambda_re, ssm_lambda_im, ssm_log_dt,
           ssm_b_re, ssm_b_im, ssm_c_re, ssm_c_im, ssm_d, w_glu, b_glu, attn_out_g,
           ssm_out_g, w_out, norm2_g, w_mlp_in, w_mlp_out):
    B, L, D = x.shape
    assert (D, L % TM_PROJ, L % TM_MLP, L % TT) == (D_MODEL, 0, 0, 0)
    heads = SB_WIDTH // HEAD_DIM
    row = lambda g: g.astype(F32).reshape(1, -1)

    head_id = jnp.arange(SB_WIDTH) // HEAD_DIM
    hsel = (head_id[:, None] == head_id[None, :]).astype(BF16) * (1.0 / HEAD_DIM)
    gq = row(jnp.tile(q_norm_g.astype(F32), heads)) * (1.0 / math.sqrt(HEAD_DIM))
    gk = row(jnp.tile(k_norm_g.astype(F32), heads))

    qt, k, vt, u = _inproj(x.astype(F32), row(norm1_g), w_in.astype(BF16), hsel, gq, gk)

    bre, bim, are, aim, cre, cim = _s5_params(
        ssm_lambda_re, ssm_lambda_im, ssm_log_dt, ssm_b_re, ssm_b_im, ssm_c_re, ssm_c_im)
    ssm = _s5(u, bre, bim, are, aim, cre, cim,
              row(ssm_d), w_glu.astype(BF16), row(b_glu), row(ssm_out_g))

    out = _attn_mlp(qt, k, vt, x.astype(F32), ssm, row(attn_out_g), w_out.astype(BF16),
                    row(norm2_g), w_mlp_in.astype(BF16), w_mlp_out.astype(BF16))
    return out.astype(x.dtype)
```

```python
import functools
import math

import jax
import jax.numpy as jnp
from jax import lax
from jax.experimental import pallas as pl
from jax.experimental.pallas import tpu as pltpu

F32 = jnp.float32
BF16 = jnp.bfloat16

D_MODEL = 1024
SB_WIDTH = 512
HEAD_DIM = 64
HEAD_PAIRS = SB_WIDTH // (2 * HEAD_DIM)
SSM_WIDTH = 512
SSM_GROUP = 16
SSM_GROUPS = 32
SSM_STATE = 64
SSM_REAL = SSM_GROUPS * SSM_STATE
D_FF = 4 * D_MODEL
EPS = 1e-6

LANES = 128
MXU_DIM = 256
VMEM_LIMIT_BYTES = 56 * 1024 * 1024

TM_PROJ = 1024
TQ = 256
TK = 128
TT = 32
SCAN_W = 512
TM_MLP = 512
FF_SUB = 512

SKIP_THRESHOLD = 88.0
N_FAR = 1
NO_KEYS = 1e30


def _rms(x):
    return x * lax.rsqrt(jnp.mean(x * x, axis=-1, keepdims=True) + EPS)


def _inproj_kernel(x_ref, g1_ref, win_ref, hsel_ref, gq_ref, gk_ref,
                   qt_ref, k_ref, vt_ref, u_ref):
    x = x_ref[0]
    xn = (_rms(x) * g1_ref[...]).astype(BF16)
    proj = jnp.dot(xn, win_ref[...], preferred_element_type=F32)
    q = proj[:, 0 * SB_WIDTH:1 * SB_WIDTH]
    k = proj[:, 1 * SB_WIDTH:2 * SB_WIDTH]
    v = proj[:, 2 * SB_WIDTH:3 * SB_WIDTH]
    u = proj[:, 3 * SB_WIDTH:]
    hsel = hsel_ref[...]
    msq = jnp.dot((q * q).astype(BF16), hsel, preferred_element_type=F32)
    msk = jnp.dot((k * k).astype(BF16), hsel, preferred_element_type=F32)
    qn = q * lax.rsqrt(msq + EPS) * gq_ref[...]
    kn = k * lax.rsqrt(msk + EPS) * gk_ref[...]
    k_ref[0] = kn.astype(BF16)
    u_ref[0] = u.astype(BF16)
    for j in range(TM_PROJ // TQ):
        qt = qn[j * TQ:(j + 1) * TQ, :].T
        qt_ref[0, :, j] = qt.astype(BF16).reshape(HEAD_PAIRS, LANES, TQ)
    for j in range(TM_PROJ // TK):
        vt = v[j * TK:(j + 1) * TK, :].T
        vt_ref[0, :, j] = vt.astype(BF16).reshape(HEAD_PAIRS, LANES, TK)


def _inproj(x, g1, win, hsel, gq, gk):
    B, L, D = x.shape
    nq, nk = L // TQ, L // TK
    const = lambda b, t: (0, 0)
    return pl.pallas_call(
        _inproj_kernel,
        grid=(B, L // TM_PROJ),
        in_specs=[
            pl.BlockSpec((1, TM_PROJ, D), lambda b, t: (b, t, 0)),
            pl.BlockSpec((1, D), const),
            pl.BlockSpec((D, 4 * SB_WIDTH), const),
            pl.BlockSpec((SB_WIDTH, SB_WIDTH), const),
            pl.BlockSpec((1, SB_WIDTH), const),
            pl.BlockSpec((1, SB_WIDTH), const),
        ],
        out_specs=[
            pl.BlockSpec((1, HEAD_PAIRS, TM_PROJ // TQ, LANES, TQ), lambda b, t: (b, 0, t, 0, 0)),
            pl.BlockSpec((1, TM_PROJ, SB_WIDTH), lambda b, t: (b, t, 0)),
            pl.BlockSpec((1, HEAD_PAIRS, TM_PROJ // TK, LANES, TK), lambda b, t: (b, 0, t, 0, 0)),
            pl.BlockSpec((1, TM_PROJ, SSM_WIDTH), lambda b, t: (b, t, 0)),
        ],
        out_shape=[
            jax.ShapeDtypeStruct((B, HEAD_PAIRS, nq, LANES, TQ), BF16),
            jax.ShapeDtypeStruct((B, L, SB_WIDTH), BF16),
            jax.ShapeDtypeStruct((B, HEAD_PAIRS, nk, LANES, TK), BF16),
            jax.ShapeDtypeStruct((B, L, SSM_WIDTH), BF16),
        ],
        compiler_params=pltpu.CompilerParams(
            dimension_semantics=("parallel", "parallel"),
            vmem_limit_bytes=VMEM_LIMIT_BYTES),
        name="inproj",
    )(x, g1, win, hsel, gq, gk)


def _gelu_tanh(x):
    c = math.sqrt(2.0 / math.pi)
    return 0.5 * x * (1.0 + jnp.tanh(c * (x + 0.044715 * (x * x * x))))


def _s5_kernel(u_ref, perm_ref, permt_ref, bre_ref, bim_ref, are_ref, aim_ref, cre_ref, cim_ref,
               d_ref, wglu_ref, bglu_ref, gs_ref, o_ref, ub_ref, bu_ref, xs_ref, st_ref, *, batch):
    s = pl.program_id(0)

    @pl.when(s == 0)
    def _():
        for ref in (ub_ref, bu_ref, xs_ref, st_ref):
            ref[...] = jnp.zeros_like(ref)

    rows = batch * TT
    half = SSM_REAL // 2
    cur = lax.rem(s, 2)
    prev = 1 - cur

    def stage1_b_bar_u(u, kb):
        ub = u[:, kb * MXU_DIM:(kb + 1) * MXU_DIM]
        bu_ref[cur, :, kb * half:(kb + 1) * half] = jnp.dot(
            ub, bre_ref[kb], preferred_element_type=F32)
        bu_ref[cur, :, SSM_REAL + kb * half:SSM_REAL + (kb + 1) * half] = jnp.dot(
            ub, bim_ref[kb], preferred_element_type=F32)

    ys = []
    for ob in range(2):
        y = jnp.dot(xs_ref[cur, :, ob * half:(ob + 1) * half], cre_ref[ob],
                    preferred_element_type=F32)
        y = y + jnp.dot(xs_ref[cur, :, SSM_REAL + ob * half:SSM_REAL + (ob + 1) * half],
                        cim_ref[ob], preferred_element_type=F32)
        ys.append(y)
    y = jnp.concatenate(ys, axis=1) + d_ref[...] * ub_ref[lax.rem(s + 1, 3)].astype(F32)

    u = jnp.dot(perm_ref[...], u_ref[...].reshape(rows, SSM_WIDTH),
                preferred_element_type=F32).astype(BF16)
    ub_ref[lax.rem(s, 3)] = u
    stage1_b_bar_u(u, 0)

    y = _gelu_tanh(y)
    gate = jax.nn.sigmoid(
        jnp.dot(y.astype(BF16), wglu_ref[...], preferred_element_type=F32) + bglu_ref[...])
    stage1_b_bar_u(u, 1)

    out_tb = (_rms(y * gate) * gs_ref[...]).astype(BF16)
    out_bt = jnp.dot(permt_ref[...], out_tb, preferred_element_type=F32)
    o_ref[...] = out_bt.astype(o_ref.dtype).reshape(batch, TT, SSM_WIDTH)

    for sc in range(SSM_REAL // SCAN_W):
        re = slice(sc * SCAN_W, (sc + 1) * SCAN_W)
        im = slice(SSM_REAL + sc * SCAN_W, SSM_REAL + (sc + 1) * SCAN_W)
        ar = jnp.broadcast_to(are_ref[:, re], (batch, SCAN_W))
        ai = jnp.broadcast_to(aim_ref[:, re], (batch, SCAN_W))

        def step(t, carry, re=re, im=im, ar=ar, ai=ai):
            xr, xi = carry
            rs = pl.ds(pl.multiple_of(t * batch, batch), batch)
            nxr = ar * xr - ai * xi + bu_ref[prev, rs, re]
            nxi = ar * xi + ai * xr + bu_ref[prev, rs, im]
            xs_ref[prev, rs, re] = nxr.astype(BF16)
            xs_ref[prev, rs, im] = nxi.astype(BF16)
            return nxr, nxi

        xr, xi = lax.fori_loop(0, TT, step, (st_ref[:, re], st_ref[:, im]), unroll=4)
        st_ref[:, re] = xr
        st_ref[:, im] = xi


def _s5(u, bre, bim, are, aim, cre, cim, d, wglu, bglu, gs):
    batch, L, _ = u.shape
    rows = TT * batch
    tiles = L // TT
    half = SSM_REAL // 2
    c2 = lambda s: (0, 0)
    c3 = lambda s: (0, 0, 0)
    src = (jnp.arange(rows) % batch) * TT + jnp.arange(rows) // batch
    perm = (src[:, None] == jnp.arange(rows)[None, :]).astype(BF16)
    return pl.pallas_call(
        functools.partial(_s5_kernel, batch=batch),
        grid=(tiles + 2,),
        in_specs=[
            pl.BlockSpec((batch, TT, SSM_WIDTH), lambda s: (0, jnp.minimum(s, tiles - 1), 0)),
            pl.BlockSpec((rows, rows), c2),
            pl.BlockSpec((rows, rows), c2),
            pl.BlockSpec((2, MXU_DIM, half), c3),
            pl.BlockSpec((2, MXU_DIM, half), c3),
            pl.BlockSpec((1, SSM_REAL), c2),
            pl.BlockSpec((1, SSM_REAL), c2),
            pl.BlockSpec((2, half, MXU_DIM), c3),
            pl.BlockSpec((2, half, MXU_DIM), c3),
            pl.BlockSpec((1, SSM_WIDTH), c2),
            pl.BlockSpec((SSM_WIDTH, SSM_WIDTH), c2),
            pl.BlockSpec((1, SSM_WIDTH), c2),
            pl.BlockSpec((1, SSM_WIDTH), c2),
        ],
        out_specs=pl.BlockSpec((batch, TT, SSM_WIDTH), lambda s: (0, jnp.maximum(s - 2, 0), 0)),
        out_shape=jax.ShapeDtypeStruct((batch, L, SSM_WIDTH), BF16),
        scratch_shapes=[
            pltpu.VMEM((3, rows, SSM_WIDTH), BF16),
            pltpu.VMEM((2, rows, 2 * SSM_REAL), F32),
            pltpu.VMEM((2, rows, 2 * SSM_REAL), BF16),
            pltpu.VMEM((batch, 2 * SSM_REAL), F32),
        ],
        compiler_params=pltpu.CompilerParams(
            dimension_semantics=("arbitrary",),
            vmem_limit_bytes=VMEM_LIMIT_BYTES),
        name="s5_glu",
    )(u, perm, perm.T, bre, bim, are, aim, cre, cim, d, wglu, bglu, gs)


def _attention_stages(qt_ref, k_ref, vt_ref, mlo_ref, z0_ref, sp0_ref, z_ref, sp_ref, acc_ref,
                      r_ref, rall_ref, rmin_ref, write_tile, add_tile):
    srow = lax.broadcasted_iota(jnp.int32, (TK, TK), 0)
    scol = lax.broadcasted_iota(jnp.int32, (TK, TK), 1)
    tri = jnp.where(scol >= srow, 1.0, 0.0).astype(BF16)
    drow = lax.broadcasted_iota(jnp.int32, (LANES, TQ), 0)

    def init():
        row = lax.broadcasted_iota(jnp.int32, (TK, TQ), 0)
        col = lax.broadcasted_iota(jnp.int32, (TK, TQ), 1)
        mlo_ref[...] = jnp.where(row < col, 1.0, 0.0)

    def weights(z, s, mask):
        return (jnp.exp(jnp.minimum(z - s, 0.0)) * mask).astype(BF16)

    def k_block(kb):
        return k_ref[0, pl.ds(pl.multiple_of(kb * TK, TK), TK), :]

    def softplus(z):
        z = z.astype(BF16)
        return jnp.maximum(z, 0.0) + jnp.log(1.0 + jnp.exp(-jnp.abs(z)))

    def cumsum(sp):
        return jnp.dot(tri, sp, preferred_element_type=F32)

    def head_queries(qi):
        qt = qt_ref[0, 0, qi]
        zero = jnp.zeros_like(qt)
        return (jnp.where(drow < HEAD_DIM, qt, zero), jnp.where(drow >= HEAD_DIM, qt, zero))

    def far_kb(qi, j):
        return jnp.maximum(2 * qi - 1 - j, 0)

    def stage_a(qi, slot):
        qh = head_queries(qi)
        k_top, k_lo = k_block(2 * qi + 1), k_block(2 * qi)
        k_far = [k_block(far_kb(qi, j)) for j in range(N_FAR)]
        for h in range(2):
            z0 = jnp.dot(k_top, qh[h][:, TK:], preferred_element_type=F32)
            z0_ref[slot, h] = z0
            sp0_ref[slot, h] = softplus(z0) * mlo_ref[:, :TK].astype(BF16)
            z1 = jnp.dot(k_lo, qh[h], preferred_element_type=F32)
            z_ref[slot, h, 0] = z1
            sp_ref[slot, h, 0] = softplus(z1) * mlo_ref[...].astype(BF16)
            for j in range(N_FAR):
                z = jnp.dot(k_far[j], qh[h], preferred_element_type=F32)
                z_ref[slot, h, 1 + j] = z
                sp_ref[slot, h, 1 + j] = softplus(z)

    def stage_b1(slot):
        return [(cumsum(sp0_ref[slot, h]), [cumsum(sp_ref[slot, h, j]) for j in range(1 + N_FAR)])
                for h in range(2)]

    def stage_b2(qi, slot, sums):
        kbs = [2 * qi + 1, 2 * qi] + [far_kb(qi, j) for j in range(N_FAR)]
        vs = [vt_ref[0, 0, kb] for kb in kbs]
        no_far = jnp.where(qi == 0, NO_KEYS, 0.0).astype(F32)
        rmin, accs = None, []
        for h in range(2):
            c0, cs = sums[h]
            a0 = weights(z0_ref[slot, h], c0, mlo_ref[:, :TK])
            r = jnp.concatenate([jnp.zeros((1, TK), F32), c0[0:1, :]], axis=1)
            a1 = weights(z_ref[slot, h, 0], cs[0] + r, mlo_ref[...])
            r = r + cs[0][0:1, :] + no_far
            parts = [jnp.concatenate([jnp.zeros((TK, TK), BF16), a0], axis=1), a1]
            for j in range(N_FAR):
                c = cs[1 + j]
                parts.append(jnp.exp(z_ref[slot, h, 1 + j] - (c + r)).astype(BF16))
                r = r + c[0:1, :]
            hs = slice(h * HEAD_DIM, (h + 1) * HEAD_DIM)
            v_cat = jnp.concatenate([v[hs, :] for v in vs], axis=1)
            accs.append(jnp.dot(v_cat, jnp.concatenate(parts, axis=0),
                                preferred_element_type=F32))
            rall_ref[qi, h] = r
            m = jnp.min(r)
            rmin = m if rmin is None else jnp.minimum(rmin, m)
        rmin_ref[qi] = rmin
        write_tile(qi, jnp.concatenate(accs, axis=0).T)

    def far_block(kb, qh):
        kblk = k_block(kb)
        vblk = vt_ref[0, 0, kb]
        rmin = None
        zs = [jnp.dot(kblk, qh[h], preferred_element_type=F32) for h in range(2)]
        cs = [cumsum(softplus(z)) for z in zs]
        for h in range(2):
            z, c = zs[h], cs[h]
            r_old = r_ref[h]
            a = jnp.exp(z - (c + r_old)).astype(BF16)
            acc_ref[h] += jnp.dot(vblk[h * HEAD_DIM:(h + 1) * HEAD_DIM, :], a,
                                  preferred_element_type=F32)
            r_new = r_old + c[0:1, :]
            r_ref[h] = r_new
            m = jnp.min(r_new)
            rmin = m if rmin is None else jnp.minimum(rmin, m)
        return rmin

    def finish(nq):
        def one_tile(qi, carry):
            kb0 = 2 * qi - 1 - N_FAR
            rmin0 = rmin_ref[qi]

            @pl.when(jnp.logical_and(kb0 >= 0, rmin0 < SKIP_THRESHOLD))
            def _():
                qh = head_queries(qi)
                acc_ref[...] = jnp.zeros_like(acc_ref)
                for h in range(2):
                    r_ref[h] = rall_ref[qi, h]

                def cond(c):
                    kb, rmin = c
                    return jnp.logical_and(kb >= 0, rmin < SKIP_THRESHOLD)

                def body(c):
                    kb, _ = c
                    return kb - 1, far_block(kb, qh)

                lax.while_loop(cond, body, (kb0, rmin0))
                acc = jnp.concatenate([acc_ref[0], acc_ref[1]], axis=0)
                add_tile(qi, acc.T)

            return carry

        lax.fori_loop(0, nq, one_tile, 0)

    return init, stage_a, stage_b1, stage_b2, finish


def _attn_mlp_kernel(qt_ref, k_ref, vt_ref, x_ref, ssm_ref, ga_ref, wout_ref, g2_ref, w1_ref, w2_ref,
                     o_ref, sb_ref, hn_ref, mlo_ref, z0_ref, sp0_ref, z_ref, sp_ref, acc_ref, r_ref,
                     rall_ref, rmin_ref):
    s, t = pl.program_id(0), pl.program_id(1)
    last = pl.num_programs(0) - 1
    nq = qt_ref.shape[2]
    wslot = lax.rem(s, 2)
    rslot = 1 - wslot

    def tile_rows(qi):
        return pl.ds(pl.multiple_of(qi * TQ, TQ), TQ)

    def write_tile(qi, tile):
        sb_ref[wslot, t, tile_rows(qi), :] = tile

    def add_tile(qi, tile):
        sb_ref[wslot, t, tile_rows(qi), :] += tile

    init, stage_a, stage_b1, stage_b2, finish = _attention_stages(
        qt_ref, k_ref, vt_ref, mlo_ref, z0_ref, sp0_ref, z_ref, sp_ref, acc_ref, r_ref, rall_ref,
        rmin_ref, write_tile, add_tile)

    def mlp_head():
        rows = pl.ds(pl.multiple_of(t * TM_MLP, TM_MLP), TM_MLP)
        sb = jnp.concatenate([sb_ref[rslot, p, rows, :] for p in range(HEAD_PAIRS)], axis=1)
        an = (_rms(sb) * ga_ref[...]).astype(BF16)
        h = x_ref[0] + jnp.dot(an, wout_ref[:SB_WIDTH, :], preferred_element_type=F32)
        h = h + jnp.dot(ssm_ref[0], wout_ref[SB_WIDTH:, :], preferred_element_type=F32)
        hn_ref[...] = (_rms(h) * g2_ref[...]).astype(BF16)
        o_ref[0] = h

    def mlp_chunk(c):
        cols = slice(c * FF_SUB, (c + 1) * FF_SUB)
        a = jnp.dot(hn_ref[...], w1_ref[:, cols], preferred_element_type=F32)
        a = jnp.square(jnp.maximum(a, 0.0)).astype(BF16)
        o_ref[0] += jnp.dot(a, w2_ref[cols, :], preferred_element_type=F32)

    def step(attn, mlp):
        if attn:
            init()
            stage_a(0, 0)
        if mlp:
            mlp_head()

        def tile(qi, slot, next_slot, has_next=True):
            if attn:
                sums = stage_b1(slot)
                if has_next:
                    stage_a(qi + 1, next_slot)
            if mlp:
                mlp_chunk(qi)
            if attn:
                stage_b2(qi, slot, sums)

        for qi in range(nq):
            tile(qi, qi % 2, 1 - qi % 2, has_next=qi + 1 < nq)
        if attn:
            finish(nq)

    @pl.when(s == 0)
    def _():
        step(True, False)

    @pl.when(jnp.logical_and(s > 0, s < last))
    def _():
        step(True, True)

    @pl.when(s == last)
    def _():
        step(False, True)


def _attn_mlp(qt, k, vt, x, ssm, ga, wout, g2, w1, w2):
    B, _, nq, _, _ = qt.shape
    L, D = x.shape[1], x.shape[2]
    nk = vt.shape[2]
    assert nq % 2 == 0 and nq >= 2 and nq == D_FF // FF_SUB and L // TM_MLP == HEAD_PAIRS
    const = lambda s, t: (0, 0)
    cur = lambda s: jnp.minimum(s, B - 1)
    prv = lambda s: jnp.maximum(s - 1, 0)
    tok = lambda s, t: jnp.where(s == 0, 0, t)
    return pl.pallas_call(
        _attn_mlp_kernel,
        grid=(B + 1, HEAD_PAIRS),
        in_specs=[
            pl.BlockSpec((1, 1, nq, LANES, TQ), lambda s, t: (cur(s), t, 0, 0, 0)),
            pl.BlockSpec((1, L, LANES), lambda s, t: (cur(s), 0, t)),
            pl.BlockSpec((1, 1, nk, LANES, TK), lambda s, t: (cur(s), t, 0, 0, 0)),
            pl.BlockSpec((1, TM_MLP, D), lambda s, t: (prv(s), tok(s, t), 0)),
            pl.BlockSpec((1, TM_MLP, SSM_WIDTH), lambda s, t: (prv(s), tok(s, t), 0)),
            pl.BlockSpec((1, SB_WIDTH), const),
            pl.BlockSpec((D, D), const),
            pl.BlockSpec((1, D), const),
            pl.BlockSpec((D, D_FF), const),
            pl.BlockSpec((D_FF, D), const),
        ],
        out_specs=pl.BlockSpec((1, TM_MLP, D), lambda s, t: (prv(s), tok(s, t), 0)),
        out_shape=jax.ShapeDtypeStruct((B, L, D), F32),
        scratch_shapes=[
            pltpu.VMEM((2, HEAD_PAIRS, L, LANES), F32),
            pltpu.VMEM((TM_MLP, D), BF16),
            pltpu.VMEM((TK, TQ), F32),
            pltpu.VMEM((2, 2, TK, TK), F32),
            pltpu.VMEM((2, 2, TK, TK), BF16),
            pltpu.VMEM((2, 2, 1 + N_FAR, TK, TQ), F32),
            pltpu.VMEM((2, 2, 1 + N_FAR, TK, TQ), BF16),
            pltpu.VMEM((2, HEAD_DIM, TQ), F32),
            pltpu.VMEM((2, 1, TQ), F32),
            pltpu.VMEM((nq, 2, 1, TQ), F32),
            pltpu.SMEM((nq,), F32),
        ],
        compiler_params=pltpu.CompilerParams(
            dimension_semantics=("arbitrary", "arbitrary"),
            vmem_limit_bytes=VMEM_LIMIT_BYTES),
        name="attn_mlp",
    )(qt, k, vt, x, ssm, ga, wout, g2, w1, w2)


def _s5_params(lam_re, lam_im, log_dt, b_re, b_im, c_re, c_im):
    G, P, H = SSM_GROUPS, SSM_STATE, SSM_GROUP
    lr, li = lam_re.astype(F32), lam_im.astype(F32)
    dt = jnp.exp(log_dt.astype(F32))[:, None]
    mag = jnp.exp(lr * dt)
    are, aim = mag * jnp.cos(li * dt), mag * jnp.sin(li * dt)
    den = lr * lr + li * li
    wr = ((are - 1.0) * lr + aim * li) / den
    wi = (aim * lr - (are - 1.0) * li) / den
    bbr = wr[:, :, None] * b_re.astype(F32) - wi[:, :, None] * b_im.astype(F32)
    bbi = wr[:, :, None] * b_im.astype(F32) + wi[:, :, None] * b_re.astype(F32)
    are, aim = are.reshape(1, G * P), aim.reshape(1, G * P)
    gpb = MXU_DIM // H
    eye = jnp.eye(gpb, dtype=F32)

    def b_layout(b):
        b = b.reshape(G // gpb, gpb, P, H)
        return jnp.einsum("kgph,gf->kghfp", b, eye).reshape(G // gpb, gpb * H, gpb * P).astype(BF16)

    def c_layout(c):
        c = c.reshape(G // gpb, gpb, H, P)
        return jnp.einsum("kghp,gf->kgpfh", c, eye).reshape(G // gpb, gpb * P, gpb * H).astype(BF16)

    return (b_layout(bbr), b_layout(bbi), are, aim,
            c_layout(c_re.astype(F32)), c_layout(-c_im.astype(F32)))


def kernel(x, norm1_g, w_in, q_norm_g, k_norm_g, ssm_lambda_re, ssm_lambda_im, ssm_log_dt,
           ssm_b_re, ssm_b_im, ssm_c_re, ssm_c_im, ssm_d, w_glu, b_glu, attn_out_g,
           ssm_out_g, w_out, norm2_g, w_mlp_in, w_mlp_out):
    B, L, D = x.shape
    assert (D, L % TM_PROJ, L % TM_MLP, L % TT) == (D_MODEL, 0, 0, 0)
    heads = SB_WIDTH // HEAD_DIM
    row = lambda g: g.astype(F32).reshape(1, -1)

    head_id = jnp.arange(SB_WIDTH) // HEAD_DIM
    hsel = (head_id[:, None] == head_id[None, :]).astype(BF16) * (1.0 / HEAD_DIM)
    gq = row(jnp.tile(q_norm_g.astype(F32), heads)) * (1.0 / math.sqrt(HEAD_DIM))
    gk = row(jnp.tile(k_norm_g.astype(F32), heads))

    qt, k, vt, u = _inproj(x.astype(F32), row(norm1_g), w_in.astype(BF16), hsel, gq, gk)

    bre, bim, are, aim, cre, cim = _s5_params(
        ssm_lambda_re, ssm_lambda_im, ssm_log_dt, ssm_b_re, ssm_b_im, ssm_c_re, ssm_c_im)
    ssm = _s5(u, bre, bim, are, aim, cre, cim,
              row(ssm_d), w_glu.astype(BF16), row(b_glu), row(ssm_out_g))

    out = _attn_mlp(qt, k, vt, x.astype(F32), ssm, row(attn_out_g), w_out.astype(BF16),
                    row(norm2_g), w_mlp_in.astype(BF16), w_mlp_out.astype(BF16))
    return out.astype(x.dtype)
```

```python
import functools
import math

import jax
import jax.numpy as jnp
from jax import lax
from jax.experimental import pallas as pl
from jax.experimental.pallas import tpu as pltpu

F32 = jnp.float32
BF16 = jnp.bfloat16

D_MODEL = 1024
SB_WIDTH = 512
HEAD_DIM = 64
HEAD_PAIRS = SB_WIDTH // (2 * HEAD_DIM)
SSM_WIDTH = 512
SSM_GROUP = 16
SSM_GROUPS = 32
SSM_STATE = 64
SSM_REAL = SSM_GROUPS * SSM_STATE
D_FF = 4 * D_MODEL
EPS = 1e-6

LANES = 128
MXU_DIM = 256
VMEM_LIMIT_BYTES = 56 * 1024 * 1024

TM_PROJ = 1024
TQ = 256
TK = 128
TT = 32
SCAN_W = 512
TM_MLP = 512
FF_SUB = 512

SKIP_THRESHOLD = 88.0
N_FAR = 1
NO_KEYS = 1e30


def _rms(x):
    return x * lax.rsqrt(jnp.mean(x * x, axis=-1, keepdims=True) + EPS)


def _inproj_kernel(x_ref, g1_ref, win_ref, hsel_ref, gq_ref, gk_ref,
                   qt_ref, k_ref, vt_ref, u_ref):
    x = x_ref[0]
    xn = (_rms(x) * g1_ref[...]).astype(BF16)
    proj = jnp.dot(xn, win_ref[...], preferred_element_type=F32)
    q = proj[:, 0 * SB_WIDTH:1 * SB_WIDTH]
    k = proj[:, 1 * SB_WIDTH:2 * SB_WIDTH]
    v = proj[:, 2 * SB_WIDTH:3 * SB_WIDTH]
    u = proj[:, 3 * SB_WIDTH:]
    def head_mean(sq):
        sq = sq.astype(BF16)
        return jnp.concatenate(
            [jnp.dot(sq[:, c:c + MXU_DIM], hsel_ref[c:c + MXU_DIM, c:c + MXU_DIM],
                     preferred_element_type=F32) for c in range(0, SB_WIDTH, MXU_DIM)], axis=1)

    msq = head_mean(q * q)
    msk = head_mean(k * k)
    qn = q * lax.rsqrt(msq + EPS) * gq_ref[...]
    kn = k * lax.rsqrt(msk + EPS) * gk_ref[...]
    k_ref[0] = kn.astype(BF16)
    u_ref[0] = u.astype(BF16)
    for j in range(TM_PROJ // TQ):
        qt = qn[j * TQ:(j + 1) * TQ, :].T
        qt_ref[0, :, j] = qt.astype(BF16).reshape(HEAD_PAIRS, LANES, TQ)
    for j in range(TM_PROJ // TK):
        vt = v[j * TK:(j + 1) * TK, :].T
        vt_ref[0, :, j] = vt.astype(BF16).reshape(HEAD_PAIRS, LANES, TK)


def _inproj(x, g1, win, hsel, gq, gk):
    B, L, D = x.shape
    nq, nk = L // TQ, L // TK
    const = lambda b, t: (0, 0)
    return pl.pallas_call(
        _inproj_kernel,
        grid=(B, L // TM_PROJ),
        in_specs=[
            pl.BlockSpec((1, TM_PROJ, D), lambda b, t: (b, t, 0)),
            pl.BlockSpec((1, D), const),
            pl.BlockSpec((D, 4 * SB_WIDTH), const),
            pl.BlockSpec((SB_WIDTH, SB_WIDTH), const),
            pl.BlockSpec((1, SB_WIDTH), const),
            pl.BlockSpec((1, SB_WIDTH), const),
        ],
        out_specs=[
            pl.BlockSpec((1, HEAD_PAIRS, TM_PROJ // TQ, LANES, TQ), lambda b, t: (b, 0, t, 0, 0)),
            pl.BlockSpec((1, TM_PROJ, SB_WIDTH), lambda b, t: (b, t, 0)),
            pl.BlockSpec((1, HEAD_PAIRS, TM_PROJ // TK, LANES, TK), lambda b, t: (b, 0, t, 0, 0)),
            pl.BlockSpec((1, TM_PROJ, SSM_WIDTH), lambda b, t: (b, t, 0)),
        ],
        out_shape=[
            jax.ShapeDtypeStruct((B, HEAD_PAIRS, nq, LANES, TQ), BF16),
            jax.ShapeDtypeStruct((B, L, SB_WIDTH), BF16),
            jax.ShapeDtypeStruct((B, HEAD_PAIRS, nk, LANES, TK), BF16),
            jax.ShapeDtypeStruct((B, L, SSM_WIDTH), BF16),
        ],
        compiler_params=pltpu.CompilerParams(
            dimension_semantics=("parallel", "parallel"),
            vmem_limit_bytes=VMEM_LIMIT_BYTES),
        name="inproj",
    )(x, g1, win, hsel, gq, gk)


def _gelu_tanh(x):
    c = math.sqrt(2.0 / math.pi)
    return 0.5 * x * (1.0 + jnp.tanh(c * (x + 0.044715 * (x * x * x))))


def _s5_kernel(u_ref, perm_ref, permt_ref, bre_ref, bim_ref, are_ref, aim_ref, cre_ref, cim_ref,
               d_ref, wglu_ref, bglu_ref, gs_ref, o_ref, ub_ref, bu_ref, xs_ref, st_ref, *, batch):
    s = pl.program_id(0)

    @pl.when(s == 0)
    def _():
        for ref in (ub_ref, bu_ref, xs_ref, st_ref):
            ref[...] = jnp.zeros_like(ref)

    rows = batch * TT
    half = SSM_REAL // 2
    cur = lax.rem(s, 2)
    prev = 1 - cur

    def stage1_b_bar_u(u, kb):
        ub = u[:, kb * MXU_DIM:(kb + 1) * MXU_DIM]
        bu_ref[cur, :, kb * half:(kb + 1) * half] = jnp.dot(
            ub, bre_ref[kb], preferred_element_type=F32)
        bu_ref[cur, :, SSM_REAL + kb * half:SSM_REAL + (kb + 1) * half] = jnp.dot(
            ub, bim_ref[kb], preferred_element_type=F32)

    ys = []
    for ob in range(2):
        y = jnp.dot(xs_ref[cur, :, ob * half:(ob + 1) * half], cre_ref[ob],
                    preferred_element_type=F32)
        y = y + jnp.dot(xs_ref[cur, :, SSM_REAL + ob * half:SSM_REAL + (ob + 1) * half],
                        cim_ref[ob], preferred_element_type=F32)
        ys.append(y)
    y = jnp.concatenate(ys, axis=1) + d_ref[...] * ub_ref[lax.rem(s + 1, 3)].astype(F32)

    u = jnp.dot(perm_ref[...], u_ref[...].reshape(rows, SSM_WIDTH),
                preferred_element_type=F32).astype(BF16)
    ub_ref[lax.rem(s, 3)] = u
    stage1_b_bar_u(u, 0)

    y = _gelu_tanh(y)
    gate = jax.nn.sigmoid(
        jnp.dot(y.astype(BF16), wglu_ref[...], preferred_element_type=F32) + bglu_ref[...])
    stage1_b_bar_u(u, 1)

    out_tb = (_rms(y * gate) * gs_ref[...]).astype(BF16)
    out_bt = jnp.dot(permt_ref[...], out_tb, preferred_element_type=F32)
    o_ref[...] = out_bt.astype(o_ref.dtype).reshape(batch, TT, SSM_WIDTH)

    for sc in range(SSM_REAL // SCAN_W):
        re = slice(sc * SCAN_W, (sc + 1) * SCAN_W)
        im = slice(SSM_REAL + sc * SCAN_W, SSM_REAL + (sc + 1) * SCAN_W)
        ar = jnp.broadcast_to(are_ref[:, re], (batch, SCAN_W))
        ai = jnp.broadcast_to(aim_ref[:, re], (batch, SCAN_W))

        def step(t, carry, re=re, im=im, ar=ar, ai=ai):
            xr, xi = carry
            rs = pl.ds(pl.multiple_of(t * batch, batch), batch)
            nxr = ar * xr - ai * xi + bu_ref[prev, rs, re]
            nxi = ar * xi + ai * xr + bu_ref[prev, rs, im]
            xs_ref[prev, rs, re] = nxr.astype(BF16)
            xs_ref[prev, rs, im] = nxi.astype(BF16)
            return nxr, nxi

        xr, xi = lax.fori_loop(0, TT, step, (st_ref[:, re], st_ref[:, im]), unroll=4)
        st_ref[:, re] = xr
        st_ref[:, im] = xi


def _s5(u, bre, bim, are, aim, cre, cim, d, wglu, bglu, gs):
    batch, L, _ = u.shape
    rows = TT * batch
    tiles = L // TT
    half = SSM_REAL // 2
    c2 = lambda s: (0, 0)
    c3 = lambda s: (0, 0, 0)
    src = (jnp.arange(rows) % batch) * TT + jnp.arange(rows) // batch
    perm = (src[:, None] == jnp.arange(rows)[None, :]).astype(BF16)
    return pl.pallas_call(
        functools.partial(_s5_kernel, batch=batch),
        grid=(tiles + 2,),
        in_specs=[
            pl.BlockSpec((batch, TT, SSM_WIDTH), lambda s: (0, jnp.minimum(s, tiles - 1), 0)),
            pl.BlockSpec((rows, rows), c2),
            pl.BlockSpec((rows, rows), c2),
            pl.BlockSpec((2, MXU_DIM, half), c3),
            pl.BlockSpec((2, MXU_DIM, half), c3),
            pl.BlockSpec((1, SSM_REAL), c2),
            pl.BlockSpec((1, SSM_REAL), c2),
            pl.BlockSpec((2, half, MXU_DIM), c3),
            pl.BlockSpec((2, half, MXU_DIM), c3),
            pl.BlockSpec((1, SSM_WIDTH), c2),
            pl.BlockSpec((SSM_WIDTH, SSM_WIDTH), c2),
            pl.BlockSpec((1, SSM_WIDTH), c2),
            pl.BlockSpec((1, SSM_WIDTH), c2),
        ],
        out_specs=pl.BlockSpec((batch, TT, SSM_WIDTH), lambda s: (0, jnp.maximum(s - 2, 0), 0)),
        out_shape=jax.ShapeDtypeStruct((batch, L, SSM_WIDTH), BF16),
        scratch_shapes=[
            pltpu.VMEM((3, rows, SSM_WIDTH), BF16),
            pltpu.VMEM((2, rows, 2 * SSM_REAL), F32),
            pltpu.VMEM((2, rows, 2 * SSM_REAL), BF16),
            pltpu.VMEM((batch, 2 * SSM_REAL), F32),
        ],
        compiler_params=pltpu.CompilerParams(
            dimension_semantics=("arbitrary",),
            vmem_limit_bytes=VMEM_LIMIT_BYTES),
        name="s5_glu",
    )(u, perm, perm.T, bre, bim, are, aim, cre, cim, d, wglu, bglu, gs)


def _attention_stages(qt_ref, k_ref, vt_ref, mlo_ref, z0_ref, sp0_ref, z_ref, sp_ref, acc_ref,
                      r_ref, rall_ref, rmin_ref, write_tile, add_tile):
    srow = lax.broadcasted_iota(jnp.int32, (TK, TK), 0)
    scol = lax.broadcasted_iota(jnp.int32, (TK, TK), 1)
    tri = jnp.where(scol >= srow, 1.0, 0.0).astype(BF16)
    drow = lax.broadcasted_iota(jnp.int32, (LANES, TQ), 0)

    def init():
        row = lax.broadcasted_iota(jnp.int32, (TK, TQ), 0)
        col = lax.broadcasted_iota(jnp.int32, (TK, TQ), 1)
        mlo_ref[...] = jnp.where(row < col, 1.0, 0.0)

    def weights(z, s, mask):
        return (jnp.exp(jnp.minimum(z - s, 0.0)) * mask).astype(BF16)

    def k_block(kb):
        return k_ref[0, pl.ds(pl.multiple_of(kb * TK, TK), TK), :]

    def softplus(z):
        z = z.astype(BF16)
        return jnp.maximum(z, 0.0) + jnp.log(1.0 + jnp.exp(-jnp.abs(z)))

    def cumsum(sp):
        return jnp.dot(tri, sp, preferred_element_type=F32)

    def head_queries(qi):
        qt = qt_ref[0, 0, qi]
        zero = jnp.zeros_like(qt)
        return (jnp.where(drow < HEAD_DIM, qt, zero), jnp.where(drow >= HEAD_DIM, qt, zero))

    def far_kb(qi, j):
        return jnp.maximum(2 * qi - 1 - j, 0)

    def stage_a(qi, slot):
        qh = head_queries(qi)
        k_top, k_lo = k_block(2 * qi + 1), k_block(2 * qi)
        k_far = [k_block(far_kb(qi, j)) for j in range(N_FAR)]
        for h in range(2):
            z0 = jnp.dot(k_top, qh[h][:, TK:], preferred_element_type=F32)
            z0_ref[slot, h] = z0
            sp0_ref[slot, h] = softplus(z0) * mlo_ref[:, :TK].astype(BF16)
            z1 = jnp.dot(k_lo, qh[h], preferred_element_type=F32)
            z_ref[slot, h, 0] = z1
            sp_ref[slot, h, 0] = softplus(z1) * mlo_ref[...].astype(BF16)
            for j in range(N_FAR):
                z = jnp.dot(k_far[j], qh[h], preferred_element_type=F32)
                z_ref[slot, h, 1 + j] = z
                sp_ref[slot, h, 1 + j] = softplus(z)

    def stage_b1(slot):
        return [(cumsum(sp0_ref[slot, h]), [cumsum(sp_ref[slot, h, j]) for j in range(1 + N_FAR)])
                for h in range(2)]

    def stage_b2(qi, slot, sums):
        kbs = [2 * qi + 1, 2 * qi] + [far_kb(qi, j) for j in range(N_FAR)]
        vs = [vt_ref[0, 0, kb] for kb in kbs]
        no_far = jnp.where(qi == 0, NO_KEYS, 0.0).astype(F32)
        rmin, accs = None, []
        for h in range(2):
            c0, cs = sums[h]
            a0 = weights(z0_ref[slot, h], c0, mlo_ref[:, :TK])
            r = jnp.concatenate([jnp.zeros((1, TK), F32), c0[0:1, :]], axis=1)
            a1 = weights(z_ref[slot, h, 0], cs[0] + r, mlo_ref[...])
            r = r + cs[0][0:1, :] + no_far
            parts = [jnp.concatenate([jnp.zeros((TK, TK), BF16), a0], axis=1), a1]
            for j in range(N_FAR):
                c = cs[1 + j]
                parts.append(jnp.exp(z_ref[slot, h, 1 + j] - (c + r)).astype(BF16))
                r = r + c[0:1, :]
            hs = slice(h * HEAD_DIM, (h + 1) * HEAD_DIM)
            v_cat = jnp.concatenate([v[hs, :] for v in vs], axis=1)
            accs.append(jnp.dot(v_cat, jnp.concatenate(parts, axis=0),
                                preferred_element_type=F32))
            rall_ref[qi, h] = r
            m = jnp.min(r)
            rmin = m if rmin is None else jnp.minimum(rmin, m)
        rmin_ref[qi] = rmin
        write_tile(qi, jnp.concatenate(accs, axis=0).T)

    def far_block(kb, qh):
        kblk = k_block(kb)
        vblk = vt_ref[0, 0, kb]
        rmin = None
        zs = [jnp.dot(kblk, qh[h], preferred_element_type=F32) for h in range(2)]
        cs = [cumsum(softplus(z)) for z in zs]
        for h in range(2):
            z, c = zs[h], cs[h]
            r_old = r_ref[h]
            a = jnp.exp(z - (c + r_old)).astype(BF16)
            acc_ref[h] += jnp.dot(vblk[h * HEAD_DIM:(h + 1) * HEAD_DIM, :], a,
                                  preferred_element_type=F32)
            r_new = r_old + c[0:1, :]
            r_ref[h] = r_new
            m = jnp.min(r_new)
            rmin = m if rmin is None else jnp.minimum(rmin, m)
        return rmin

    def finish(nq):
        def one_tile(qi, carry):
            kb0 = 2 * qi - 1 - N_FAR
            rmin0 = rmin_ref[qi]

            @pl.when(jnp.logical_and(kb0 >= 0, rmin0 < SKIP_THRESHOLD))
            def _():
                qh = head_queries(qi)
                acc_ref[...] = jnp.zeros_like(acc_ref)
                for h in range(2):
                    r_ref[h] = rall_ref[qi, h]

                def cond(c):
                    kb, rmin = c
                    return jnp.logical_and(kb >= 0, rmin < SKIP_THRESHOLD)

                def body(c):
                    kb, _ = c
                    return kb - 1, far_block(kb, qh)

                lax.while_loop(cond, body, (kb0, rmin0))
                acc = jnp.concatenate([acc_ref[0], acc_ref[1]], axis=0)
                add_tile(qi, acc.T)

            return carry

        lax.fori_loop(0, nq, one_tile, 0)

    return init, stage_a, stage_b1, stage_b2, finish


def _attn_mlp_kernel(qt_ref, k_ref, vt_ref, x_ref, ssm_ref, ga_ref, wout_ref, g2_ref, w1_ref, w2_ref,
                     o_ref, sb_ref, hn_ref, mlo_ref, z0_ref, sp0_ref, z_ref, sp_ref, acc_ref, r_ref,
                     rall_ref, rmin_ref):
    s, t = pl.program_id(0), pl.program_id(1)
    last = pl.num_programs(0) - 1
    nq = qt_ref.shape[2]
    wslot = lax.rem(s, 2)
    rslot = 1 - wslot

    def tile_rows(qi):
        return pl.ds(pl.multiple_of(qi * TQ, TQ), TQ)

    def write_tile(qi, tile):
        sb_ref[wslot, t, tile_rows(qi), :] = tile

    def add_tile(qi, tile):
        sb_ref[wslot, t, tile_rows(qi), :] += tile

    init, stage_a, stage_b1, stage_b2, finish = _attention_stages(
        qt_ref, k_ref, vt_ref, mlo_ref, z0_ref, sp0_ref, z_ref, sp_ref, acc_ref, r_ref, rall_ref,
        rmin_ref, write_tile, add_tile)

    def mlp_head():
        rows = pl.ds(pl.multiple_of(t * TM_MLP, TM_MLP), TM_MLP)
        sb = jnp.concatenate([sb_ref[rslot, p, rows, :] for p in range(HEAD_PAIRS)], axis=1)
        an = (_rms(sb) * ga_ref[...]).astype(BF16)
        h = x_ref[0] + jnp.dot(an, wout_ref[:SB_WIDTH, :], preferred_element_type=F32)
        h = h + jnp.dot(ssm_ref[0], wout_ref[SB_WIDTH:, :], preferred_element_type=F32)
        hn_ref[...] = (_rms(h) * g2_ref[...]).astype(BF16)
        o_ref[0] = h

    def mlp_chunk(c):
        cols = slice(c * FF_SUB, (c + 1) * FF_SUB)
        a = jnp.dot(hn_ref[...], w1_ref[:, cols], preferred_element_type=F32)
        a = jnp.square(jnp.maximum(a, 0.0)).astype(BF16)
        o_ref[0] += jnp.dot(a, w2_ref[cols, :], preferred_element_type=F32)

    def step(attn, mlp):
        if attn:
            init()
            stage_a(0, 0)
        if mlp:
            mlp_head()

        def tile(qi, slot, next_slot, has_next=True):
            if attn:
                sums = stage_b1(slot)
                if has_next:
                    stage_a(qi + 1, next_slot)
            if mlp:
                mlp_chunk(qi)
            if attn:
                stage_b2(qi, slot, sums)

        for qi in range(nq):
            tile(qi, qi % 2, 1 - qi % 2, has_next=qi + 1 < nq)
        if attn:
            finish(nq)

    @pl.when(s == 0)
    def _():
        step(True, False)

    @pl.when(jnp.logical_and(s > 0, s < last))
    def _():
        step(True, True)

    @pl.when(s == last)
    def _():
        step(False, True)


def _attn_mlp(qt, k, vt, x, ssm, ga, wout, g2, w1, w2):
    B, _, nq, _, _ = qt.shape
    L, D = x.shape[1], x.shape[2]
    nk = vt.shape[2]
    assert nq % 2 == 0 and nq >= 2 and nq == D_FF // FF_SUB and L // TM_MLP == HEAD_PAIRS
    const = lambda s, t: (0, 0)
    cur = lambda s: jnp.minimum(s, B - 1)
    prv = lambda s: jnp.maximum(s - 1, 0)
    tok = lambda s, t: jnp.where(s == 0, 0, t)
    return pl.pallas_call(
        _attn_mlp_kernel,
        grid=(B + 1, HEAD_PAIRS),
        in_specs=[
            pl.BlockSpec((1, 1, nq, LANES, TQ), lambda s, t: (cur(s), t, 0, 0, 0)),
            pl.BlockSpec((1, L, LANES), lambda s, t: (cur(s), 0, t)),
            pl.BlockSpec((1, 1, nk, LANES, TK), lambda s, t: (cur(s), t, 0, 0, 0)),
            pl.BlockSpec((1, TM_MLP, D), lambda s, t: (prv(s), tok(s, t), 0)),
            pl.BlockSpec((1, TM_MLP, SSM_WIDTH), lambda s, t: (prv(s), tok(s, t), 0)),
            pl.BlockSpec((1, SB_WIDTH), const),
            pl.BlockSpec((D, D), const),
            pl.BlockSpec((1, D), const),
            pl.BlockSpec((D, D_FF), const),
            pl.BlockSpec((D_FF, D), const),
        ],
        out_specs=pl.BlockSpec((1, TM_MLP, D), lambda s, t: (prv(s), tok(s, t), 0)),
        out_shape=jax.ShapeDtypeStruct((B, L, D), F32),
        scratch_shapes=[
            pltpu.VMEM((2, HEAD_PAIRS, L, LANES), F32),
            pltpu.VMEM((TM_MLP, D), BF16),
            pltpu.VMEM((TK, TQ), F32),
            pltpu.VMEM((2, 2, TK, TK), F32),
            pltpu.VMEM((2, 2, TK, TK), BF16),
            pltpu.VMEM((2, 2, 1 + N_FAR, TK, TQ), F32),
            pltpu.VMEM((2, 2, 1 + N_FAR, TK, TQ), BF16),
            pltpu.VMEM((2, HEAD_DIM, TQ), F32),
            pltpu.VMEM((2, 1, TQ), F32),
            pltpu.VMEM((nq, 2, 1, TQ), F32),
            pltpu.SMEM((nq,), F32),
        ],
        compiler_params=pltpu.CompilerParams(
            dimension_semantics=("arbitrary", "arbitrary"),
            vmem_limit_bytes=VMEM_LIMIT_BYTES),
        name="attn_mlp",
    )(qt, k, vt, x, ssm, ga, wout, g2, w1, w2)


def _s5_params(lam_re, lam_im, log_dt, b_re, b_im, c_re, c_im):
    G, P, H = SSM_GROUPS, SSM_STATE, SSM_GROUP
    lr, li = lam_re.astype(F32), lam_im.astype(F32)
    dt = jnp.exp(log_dt.astype(F32))[:, None]
    mag = jnp.exp(lr * dt)
    are, aim = mag * jnp.cos(li * dt), mag * jnp.sin(li * dt)
    den = lr * lr + li * li
    wr = ((are - 1.0) * lr + aim * li) / den
    wi = (aim * lr - (are - 1.0) * li) / den
    bbr = wr[:, :, None] * b_re.astype(F32) - wi[:, :, None] * b_im.astype(F32)
    bbi = wr[:, :, None] * b_im.astype(F32) + wi[:, :, None] * b_re.astype(F32)
    are, aim = are.reshape(1, G * P), aim.reshape(1, G * P)
    gpb = MXU_DIM // H
    eye = jnp.eye(gpb, dtype=F32)

    def b_layout(b):
        b = b.reshape(G // gpb, gpb, P, H)
        return jnp.einsum("kgph,gf->kghfp", b, eye).reshape(G // gpb, gpb * H, gpb * P).astype(BF16)

    def c_layout(c):
        c = c.reshape(G // gpb, gpb, H, P)
        return jnp.einsum("kghp,gf->kgpfh", c, eye).reshape(G // gpb, gpb * P, gpb * H).astype(BF16)

    return (b_layout(bbr), b_layout(bbi), are, aim,
            c_layout(c_re.astype(F32)), c_layout(-c_im.astype(F32)))


def kernel(x, norm1_g, w_in, q_norm_g, k_norm_g, ssm_lambda_re, ssm_lambda_im, ssm_log_dt,
           ssm_b_re, ssm_b_im, ssm_c_re, ssm_c_im, ssm_d, w_glu, b_glu, attn_out_g,
           ssm_out_g, w_out, norm2_g, w_mlp_in, w_mlp_out):
    B, L, D = x.shape
    assert (D, L % TM_PROJ, L % TM_MLP, L % TT) == (D_MODEL, 0, 0, 0)
    heads = SB_WIDTH // HEAD_DIM
    row = lambda g: g.astype(F32).reshape(1, -1)

    head_id = jnp.arange(SB_WIDTH) // HEAD_DIM
    hsel = (head_id[:, None] == head_id[None, :]).astype(BF16) * (1.0 / HEAD_DIM)
    gq = row(jnp.tile(q_norm_g.astype(F32), heads)) * (1.0 / math.sqrt(HEAD_DIM))
    gk = row(jnp.tile(k_norm_g.astype(F32), heads))

    qt, k, vt, u = _inproj(x.astype(F32), row(norm1_g), w_in.astype(BF16), hsel, gq, gk)

    bre, bim, are, aim, cre, cim = _s5_params(
        ssm_lambda_re, ssm_lambda_im, ssm_log_dt, ssm_b_re, ssm_b_im, ssm_c_re, ssm_c_im)
    ssm = _s5(u, bre, bim, are, aim, cre, cim,
              row(ssm_d), w_glu.astype(BF16), row(b_glu), row(ssm_out_g))

    out = _attn_mlp(qt, k, vt, x.astype(F32), ssm, row(attn_out_g), w_out.astype(BF16),
                    row(norm2_g), w_mlp_in.astype(BF16), w_mlp_out.astype(BF16))
    return out.astype(x.dtype)
```
